```python
import math
import jax, jax.numpy as jnp
from jax import lax
import numpy as np

D_MODEL = 2048
BATCH = 2
SEQ = 4096
DEPTH = 2
DEC_BATCH = 128
DEC_SEQ = 8
PAST_LEN = 16384
PAGE_SIZE = 128

MLA_HEADS = 8
MLA_NOPE = 128
MLA_ROPE = 64
MLA_V = 128
Q_LORA = 512
KV_LORA = 256
CONV_WIDTH = 512
CONV_K = 3
DIFF_HEADS = 4
DIFF_HALF = 64
DIFF_V = 2 * DIFF_HALF
D_FF = 5632
ROPE_THETA = 10000.0
NORM_EPS = 1e-6
Q_BLOCK = 128
NEG_INF = -1e30

MLA_OUT = MLA_HEADS * MLA_V
DIFF_OUT = DIFF_HEADS * DIFF_V
MIX_WIDTH = MLA_OUT + CONV_WIDTH + DIFF_OUT

_SIZES = (Q_LORA, KV_LORA, MLA_ROPE, CONV_WIDTH, CONV_WIDTH, CONV_WIDTH,
          DIFF_HEADS * 2 * DIFF_HALF, 2 * DIFF_HALF, DIFF_V)
IN_COLS = sum(_SIZES)
_SPLITS = tuple(np.cumsum(_SIZES)[:-1].tolist())

kernel_name = 'hybrid_mla_conv_diffattn_decoder_step'


def rmsnorm(x, g):
    xf = x.astype(jnp.float32)
    y = xf * lax.rsqrt(jnp.mean(xf * xf, axis=-1, keepdims=True) + NORM_EPS)
    return (y * g.astype(jnp.float32)).astype(x.dtype)


def swiglu(x, wg, wu, wd):
    return (jax.nn.silu(x @ wg) * (x @ wu)) @ wd


def rope(x, pos):
    half = x.shape[-1] // 2
    freqs = ROPE_THETA ** (-jnp.arange(half, dtype=jnp.float32) / half)
    ang = pos.astype(jnp.float32)[:, None] * freqs[None, :]
    cos = jnp.cos(ang)[None, :, None, :]
    sin = jnp.sin(ang)[None, :, None, :]
    x1 = x[..., :half].astype(jnp.float32)
    x2 = x[..., half:].astype(jnp.float32)
    return jnp.concatenate([x1 * cos - x2 * sin, x1 * sin + x2 * cos], axis=-1).astype(x.dtype)


def alibi_slopes():
    return jnp.asarray([2.0 ** (-8.0 * (h + 1) / DIFF_HEADS) for h in range(DIFF_HEADS)], jnp.float32)


def mla_core(q_lat, q_rope, ckv, krope, q_pos, k_pos):
    scale = (MLA_NOPE + MLA_ROPE) ** -0.5
    s = (jnp.einsum('bqhl,bkl->bhqk', q_lat, ckv).astype(jnp.float32)
         + jnp.einsum('bqhr,bkr->bhqk', q_rope, krope).astype(jnp.float32)) * scale
    causal = (k_pos[None, :] <= q_pos[:, None])[None, None]
    p = jax.nn.softmax(jnp.where(causal, s, NEG_INF), axis=-1)
    return jnp.einsum('bhqk,bkl->bqhl', p.astype(ckv.dtype), ckv)


def diff_core(q1, q2, k1, k2, v, lam, slopes, q_pos, k_pos):
    scale = DIFF_HALF ** -0.5
    dist = (q_pos[:, None] - k_pos[None, :]).astype(jnp.float32)
    bias = jnp.where(dist[None] >= 0, -slopes[:, None, None] * dist[None], NEG_INF)[None]
    s1 = jnp.einsum('bqhd,bkd->bhqk', q1, k1).astype(jnp.float32) * scale + bias
    s2 = jnp.einsum('bqhd,bkd->bhqk', q2, k2).astype(jnp.float32) * scale + bias
    p = jax.nn.softmax(s1, axis=-1) - lam * jax.nn.softmax(s2, axis=-1)
    return jnp.einsum('bhqk,bkv->bqhv', p.astype(v.dtype), v)


def over_query_blocks(fn, qs, q_pos):
    b, t = qs[0].shape[:2]
    nb = t // Q_BLOCK
    blocks = tuple(jnp.moveaxis(q.reshape((b, nb, Q_BLOCK) + q.shape[2:]), 1, 0) for q in qs)
    out = lax.map(lambda a: fn(*a), blocks + (q_pos.reshape(nb, Q_BLOCK),))
    return jnp.moveaxis(out, 0, 1).reshape((b, t) + out.shape[3:])


def layer(x, pos, w, lam_init, conv_prev, mla_attend, diff_attend):
    b, t, _ = x.shape
    h = x + 0.5 * rmsnorm(swiglu(rmsnorm(x, w['f1_pre']), w['f1_wg'], w['f1_wu'], w['f1_wd']), w['f1_post'])
    n = rmsnorm(h, w['mix_pre'])
    z = n @ w['w_in']
    cq, ckv_raw, kr_raw, cb, cc, cx, dq, dk, dv = jnp.split(z, _SPLITS, axis=-1)
    cq = rmsnorm(cq, w['q_norm'])
    q = jnp.einsum('btc,chd->bthd', cq, w['w_uq'])
    q_nope = q[..., :MLA_NOPE]
    q_rope = rope(q[..., MLA_NOPE:], pos)
    ckv = rmsnorm(ckv_raw, w['kv_norm'])
    krope = rope(kr_raw[:, :, None, :], pos)[:, :, 0, :]
    q_lat = jnp.einsum('bthn,lhn->bthl', q_nope, w['w_uk'])
    o_lat = mla_attend(q_lat, q_rope, ckv, krope)
    mla_o = jnp.einsum('bthl,lhv->bthv', o_lat, w['w_uv']).reshape(b, t, MLA_OUT)
    u = cc * cx
    cw = w['conv_w']
    pad = jnp.concatenate([conv_prev.astype(u.dtype), u], axis=1)
    conv = cw[0] * pad[:, 0:t] + cw[1] * pad[:, 1:t + 1] + cw[2] * pad[:, 2:t + 2]
    conv_o = cb * conv
    conv_state = pad[:, t:]
    lam = (jnp.exp(jnp.sum(w['lq1'].astype(jnp.float32) * w['lk1'].astype(jnp.float32)))
           - jnp.exp(jnp.sum(w['lq2'].astype(jnp.float32) * w['lk2'].astype(jnp.float32))) + lam_init)
    dq = dq.reshape(b, t, DIFF_HEADS, 2 * DIFF_HALF)
    o_d = diff_attend(dq[..., :DIFF_HALF], dq[..., DIFF_HALF:], dk, dv, lam)
    diff_o = (rmsnorm(o_d, w['diff_norm']) * (1.0 - lam_init)).reshape(b, t, DIFF_OUT)
    mixed = jnp.concatenate([mla_o, conv_o, diff_o], axis=-1) @ w['w_o']
    h = h + rmsnorm(mixed, w['mix_post'])
    h = h + 0.5 * rmsnorm(swiglu(rmsnorm(h, w['f2_pre']), w['f2_wg'], w['f2_wu'], w['f2_wd']), w['f2_post'])
    return h, (ckv, krope, dk, dv, conv_state)


def setup_inputs(seed: int = 0) -> dict:
    key = jax.random.key(seed)
    keys = list(jax.random.split(key, 40))
    f32 = jnp.float32

    def nrm(shape, scale=1.0):
        return jax.random.normal(keys.pop(), shape, f32) * scale

    def gain(shape):
        return 1.0 + 0.02 * jax.random.normal(keys.pop(), shape, f32)

    n_pages = PAST_LEN // PAGE_SIZE
    n_used = DEC_BATCH * n_pages
    n_phys = n_used + n_used // 4
    perm = jax.random.permutation(keys.pop(), n_phys)
    page_table = perm[:n_used].reshape(DEC_BATCH, n_pages).astype(jnp.int32)
    d = D_MODEL
    return {
        'x_prompt': nrm((BATCH, SEQ, d)),
        'x_sample': nrm((DEC_BATCH, DEC_SEQ, d)),
        'cache_ckv': nrm((DEPTH, n_phys, PAGE_SIZE, KV_LORA)),
        'cache_krope': nrm((DEPTH, n_phys, PAGE_SIZE, MLA_ROPE)),
        'cache_diff_k': nrm((DEPTH, n_phys, PAGE_SIZE, 2 * DIFF_HALF)),
        'cache_diff_v': nrm((DEPTH, n_phys, PAGE_SIZE, DIFF_V)),
        'state_conv': nrm((DEPTH, DEC_BATCH, CONV_K - 1, CONV_WIDTH)),
        'page_table': page_table,
        'ffn1_pre_g': gain((DEPTH, d)),
        'ffn1_post_g': gain((DEPTH, d)),
        'ffn1_w_gate': nrm((DEPTH, d, D_FF), d ** -0.5),
        'ffn1_w_up': nrm((DEPTH, d, D_FF), d ** -0.5),
        'ffn1_w_down': nrm((DEPTH, D_FF, d), D_FF ** -0.5),
        'mix_pre_g': gain((DEPTH, d)),
        'mix_post_g': gain((DEPTH, d)),
        'w_in': nrm((DEPTH, d, IN_COLS), d ** -0.5),
        'q_norm_g': gain((DEPTH, Q_LORA)),
        'kv_norm_g': gain((DEPTH, KV_LORA)),
        'w_uq': nrm((DEPTH, Q_LORA, MLA_HEADS, MLA_NOPE + MLA_ROPE), Q_LORA ** -0.5),
        'w_uk': nrm((DEPTH, KV_LORA, MLA_HEADS, MLA_NOPE), KV_LORA ** -0.5),
        'w_uv': nrm((DEPTH, KV_LORA, MLA_HEADS, MLA_V), KV_LORA ** -0.5),
        'conv_w': nrm((DEPTH, CONV_K, CONV_WIDTH), CONV_K ** -0.5),
        'diff_lq1': nrm((DEPTH, DIFF_HALF), 0.1),
        'diff_lk1': nrm((DEPTH, DIFF_HALF), 0.1),
        'diff_lq2': nrm((DEPTH, DIFF_HALF), 0.1),
        'diff_lk2': nrm((DEPTH, DIFF_HALF), 0.1),
        'diff_norm_g': gain((DEPTH, DIFF_V)),
        'w_o': nrm((DEPTH, MIX_WIDTH, d), MIX_WIDTH ** -0.5),
        'ffn2_pre_g': gain((DEPTH, d)),
        'ffn2_post_g': gain((DEPTH, d)),
        'ffn2_w_gate': nrm((DEPTH, d, D_FF), d ** -0.5),
        'ffn2_w_up': nrm((DEPTH, d, D_FF), d ** -0.5),
        'ffn2_w_down': nrm((DEPTH, D_FF, d), D_FF ** -0.5),
    }


def reference(x_prompt, x_sample, cache_ckv, cache_krope, cache_diff_k, cache_diff_v, state_conv, page_table,
              ffn1_pre_g, ffn1_post_g, ffn1_w_gate, ffn1_w_up, ffn1_w_down,
              mix_pre_g, mix_post_g, w_in, q_norm_g, kv_norm_g, w_uq, w_uk, w_uv, conv_w,
              diff_lq1, diff_lk1, diff_lq2, diff_lk2, diff_norm_g, w_o,
              ffn2_pre_g, ffn2_post_g, ffn2_w_gate, ffn2_w_up, ffn2_w_down):
    slopes = alibi_slopes()
    pos_p = jnp.arange(SEQ, dtype=jnp.int32)
    pos_s = PAST_LEN + jnp.arange(DEC_SEQ, dtype=jnp.int32)
    pos_ctx = jnp.arange(PAST_LEN + DEC_SEQ, dtype=jnp.int32)
    hp, hs = x_prompt, x_sample
    p_st, s_st = [], []
    for li in range(DEPTH):
        w = {
            'f1_pre': ffn1_pre_g[li], 'f1_post': ffn1_post_g[li],
            'f1_wg': ffn1_w_gate[li], 'f1_wu': ffn1_w_up[li], 'f1_wd': ffn1_w_down[li],
            'mix_pre': mix_pre_g[li], 'mix_post': mix_post_g[li], 'w_in': w_in[li],
            'q_norm': q_norm_g[li], 'kv_norm': kv_norm_g[li],
            'w_uq': w_uq[li], 'w_uk': w_uk[li], 'w_uv': w_uv[li], 'conv_w': conv_w[li],
            'lq1': diff_lq1[li], 'lk1': diff_lk1[li], 'lq2': diff_lq2[li], 'lk2': diff_lk2[li],
            'diff_norm': diff_norm_g[li], 'w_o': w_o[li],
            'f2_pre': ffn2_pre_g[li], 'f2_post': ffn2_post_g[li],
            'f2_wg': ffn2_w_gate[li], 'f2_wu': ffn2_w_up[li], 'f2_wd': ffn2_w_down[li],
        }
        lam_init = 0.8 - 0.6 * math.exp(-0.3 * li)

        def prompt_mla(q_lat, q_rope, ckv, krope):
            return over_query_blocks(lambda a, r, qp: mla_core(a, r, ckv, krope, qp, pos_p), (q_lat, q_rope), pos_p)

        def prompt_diff(q1, q2, k, v, lam):
            k1, k2 = k[..., :DIFF_HALF], k[..., DIFF_HALF:]
            return over_query_blocks(lambda a, c, qp: diff_core(a, c, k1, k2, v, lam, slopes, qp, pos_p), (q1, q2), pos_p)

        def sample_mla(q_lat, q_rope, ckv_new, krope_new):
            def one(a):
                pt, ql, qr, cn, kn = a
                ck = jnp.concatenate([cache_ckv[li, pt].reshape(-1, KV_LORA).astype(cn.dtype), cn], axis=0)
                kr = jnp.concatenate([cache_krope[li, pt].reshape(-1, MLA_ROPE).astype(kn.dtype), kn], axis=0)
                return mla_core(ql[None], qr[None], ck[None], kr[None], pos_s, pos_ctx)[0]
            return lax.map(one, (page_table, q_lat, q_rope, ckv_new, krope_new))

        def sample_diff(q1, q2, k_new, v_new, lam):
            def one(a):
                pt, a1, a2, kn, vn = a
                k = jnp.concatenate([cache_diff_k[li, pt].reshape(-1, 2 * DIFF_HALF).astype(kn.dtype), kn], axis=0)
                v = jnp.concatenate([cache_diff_v[li, pt].reshape(-1, DIFF_V).astype(vn.dtype), vn], axis=0)
                return diff_core(a1[None], a2[None], k[None, :, :DIFF_HALF], k[None, :, DIFF_HALF:], v[None],
                                 lam, slopes, pos_s, pos_ctx)[0]
            return lax.map(one, (page_table, q1, q2, k_new, v_new))

        conv0 = jnp.zeros((BATCH, CONV_K - 1, CONV_WIDTH), x_prompt.dtype)
        hp, st_p = layer(hp, pos_p, w, lam_init, conv0, prompt_mla, prompt_diff)
        hs, st_s = layer(hs, pos_s, w, lam_init, state_conv[li], sample_mla, sample_diff)
        p_st.append(st_p)
        s_st.append(st_s)

    p_ckv = jnp.stack([s[0] for s in p_st])
    p_krope = jnp.stack([s[1] for s in p_st])
    p_dk = jnp.stack([s[2] for s in p_st])
    p_dv = jnp.stack([s[3] for s in p_st])
    p_conv = jnp.stack([s[4] for s in p_st])
    s_ckv = jnp.stack([s[0] for s in s_st])
    s_krope = jnp.stack([s[1] for s in s_st])
    s_dk = jnp.stack([s[2] for s in s_st])
    s_dv = jnp.stack([s[3] for s in s_st])
    s_conv = jnp.stack([s[4] for s in s_st])
    return (hp, hs, p_ckv, p_krope, p_dk, p_dv, p_conv, s_ckv, s_krope, s_dk, s_dv, s_conv)
```

```python
import functools
import math

import numpy as np
import jax
import jax.numpy as jnp
from jax import lax
from jax.experimental import pallas as pl
from jax.experimental.pallas import tpu as pltpu

F32 = jnp.float32
BF16 = jnp.bfloat16

MLA_HEADS = 8
MLA_NOPE = 128
MLA_ROPE = 64
KV_LORA = 256
Q_LORA = 512
CONV_WIDTH = 512
CONV_K = 3
DIFF_HEADS = 4
DIFF_HALF = 64
DIFF_V = 2 * DIFF_HALF
ROPE_THETA = 10000.0
NORM_EPS = 1e-6
NEG_INF = -1e30
LANES = 128
ROPE_PAD = LANES
VMEM_LIMIT = 56 * 1024 * 1024

MLA_SCALE = (MLA_NOPE + MLA_ROPE) ** -0.5
DIFF_SCALE = DIFF_HALF ** -0.5

_C_CQ = 0
_C_CKV = _C_CQ + Q_LORA
_C_KR = _C_CKV + KV_LORA
_C_KRS = _C_KR + ROPE_PAD
_C_CB = _C_KRS + ROPE_PAD
_C_CC = _C_CB + CONV_WIDTH
_C_CX = _C_CC + CONV_WIDTH
_C_DQ = _C_CX + CONV_WIDTH
_C_DK = _C_DQ + DIFF_HEADS * 2 * DIFF_HALF
_C_DV = _C_DK + 2 * DIFF_HALF
_C_END = _C_DV + DIFF_V


def _params(*sem):
    return pltpu.CompilerParams(dimension_semantics=sem, vmem_limit_bytes=VMEM_LIMIT)


def _rms(x, g):
    ms = jnp.mean(x * x, axis=-1, keepdims=True)
    return x * lax.rsqrt(ms + NORM_EPS) * g


def _dot(a, b):
    return jnp.dot(a, b, preferred_element_type=F32)


def _dot_nt(a, b):
    return lax.dot_general(a, b, (((1,), (1,)), ((), ())), preferred_element_type=F32)


def _dot_tn(a, b):
    return lax.dot_general(a, b, (((0,), (0,)), ((), ())), preferred_element_type=F32)


def _lam(lq1, lk1, lq2, lk2, lam_init):
    a = jnp.sum(lq1 * lk1, axis=-1, keepdims=True)
    b = jnp.sum(lq2 * lk2, axis=-1, keepdims=True)
    return jnp.exp(a) - jnp.exp(b) + lam_init


def _const_spec(shape):
    nd = len(shape)
    return pl.BlockSpec(shape, lambda *_: (0,) * nd, pipeline_mode=pl.Buffered(1))


def _ffn_body(x_ref, pre_ref, wg_ref, wu_ref, wd_ref, post_ref, o_ref, xn_ref, *, nf):
    f = pl.program_id(1)

    @pl.when(f == 0)
    def _():
        xn_ref[...] = _rms(x_ref[...], pre_ref[...]).astype(BF16)
        o_ref[...] = jnp.zeros_like(o_ref)

    xn = xn_ref[...]
    g = _dot(xn, wg_ref[...])
    u = _dot(xn, wu_ref[...])
    a = (g * (1.0 / (1.0 + jnp.exp(-g)))) * u
    o_ref[...] += _dot(a.astype(BF16), wd_ref[...])

    @pl.when(f == nf - 1)
    def _():
        o_ref[...] = x_ref[...] + 0.5 * _rms(o_ref[...], post_ref[...])


def _ffn(x, pre_g, wg, wu, wd, post_g, tm, tf):
    m, d = x.shape
    nf = wg.shape[1] // tf
    return pl.pallas_call(
        functools.partial(_ffn_body, nf=nf),
        grid=(m // tm, nf),
        in_specs=[
            pl.BlockSpec((tm, d), lambda i, f: (i, 0)),
            pl.BlockSpec((1, d), lambda i, f: (0, 0)),
            pl.BlockSpec((d, tf), lambda i, f: (0, f)),
            pl.BlockSpec((d, tf), lambda i, f: (0, f)),
            pl.BlockSpec((tf, d), lambda i, f: (f, 0)),
            pl.BlockSpec((1, d), lambda i, f: (0, 0)),
        ],
        out_specs=pl.BlockSpec((tm, d), lambda i, f: (i, 0)),
        out_shape=jax.ShapeDtypeStruct((m, d), F32),
        scratch_shapes=[pltpu.VMEM((tm, d), BF16)],
        compiler_params=_params("arbitrary", "arbitrary"),
    )(x, pre_g, wg, wu, wd, post_g)


def _mixin_body(h_ref, pre_ref, win_ref, qn_ref, kvn_ref, wqn_ref, wqr_ref, wqs_ref, wuk_ref,
                tc_ref, ts_ref,
                ql_ref, qr_ref, ckv_ref, kr_ref, ckvb_ref, krb_ref, cb_ref, u_ref,
                dq_ref, dk_ref, dv_ref, dkb_ref, dvb_ref):
    n = _rms(h_ref[...], pre_ref[...]).astype(BF16)
    tab_c = tc_ref[...]
    tab_s = ts_ref[...]

    def proj(c0, c1):
        return _dot(n, win_ref[:, c0:c1])

    cq = _rms(proj(_C_CQ, _C_CKV), qn_ref[...]).astype(BF16)
    qn = _dot(cq, wqn_ref[...]).astype(BF16)
    qr = _dot(cq, wqr_ref[...])
    qs = _dot(cq, wqs_ref[...])
    for hh in range(MLA_HEADS):
        q_lat = _dot(qn[:, hh * MLA_NOPE:(hh + 1) * MLA_NOPE], wuk_ref[hh])
        ql_ref[hh] = (q_lat * MLA_SCALE).astype(BF16)
        sl = slice(hh * ROPE_PAD, (hh + 1) * ROPE_PAD)
        qr_ref[hh] = ((qr[:, sl] * tab_c + qs[:, sl] * tab_s) * MLA_SCALE).astype(BF16)
    ckv = _rms(proj(_C_CKV, _C_KR), kvn_ref[...])
    ckv_ref[...] = ckv
    ckvb_ref[...] = ckv.astype(BF16)
    kr = proj(_C_KR, _C_KRS) * tab_c + proj(_C_KRS, _C_CB) * tab_s
    kr_ref[...] = kr[:, :MLA_ROPE]
    krb_ref[...] = kr.astype(BF16)
    cb_ref[...] = proj(_C_CB, _C_CC)
    u_ref[...] = proj(_C_CC, _C_CX) * proj(_C_CX, _C_DQ)
    dq = proj(_C_DQ, _C_DK) * DIFF_SCALE
    for hh in range(DIFF_HEADS):
        dq_ref[hh] = dq[:, hh * 2 * DIFF_HALF:(hh + 1) * 2 * DIFF_HALF].astype(BF16)
    dk = proj(_C_DK, _C_DV)
    dv = proj(_C_DV, _C_END)
    dk_ref[...] = dk
    dv_ref[...] = dv
    dkb_ref[...] = dk.astype(BF16)
    dvb_ref[...] = dv.astype(BF16)


def _mixin(h, pre_g, win, qn_g, kvn_g, wqn, wqr, wqs, wuk, tab_c, tab_s, tm):
    m, d = h.shape
    tok = lambda w: pl.BlockSpec((tm, w), lambda i: (i, 0))
    headmajor = lambda nh, w: pl.BlockSpec((nh, tm, w), lambda i: (0, i, 0))
    sds = jax.ShapeDtypeStruct
    return pl.pallas_call(
        _mixin_body,
        grid=(m // tm,),
        in_specs=[tok(d), _const_spec((1, d)), _const_spec(win.shape), _const_spec((1, Q_LORA)),
                  _const_spec((1, KV_LORA)), _const_spec(wqn.shape), _const_spec(wqr.shape),
                  _const_spec(wqs.shape), _const_spec(wuk.shape), tok(LANES), tok(LANES)],
        out_specs=[headmajor(MLA_HEADS, KV_LORA), headmajor(MLA_HEADS, ROPE_PAD),
                   tok(KV_LORA), tok(MLA_ROPE), tok(KV_LORA), tok(ROPE_PAD),
                   tok(CONV_WIDTH), tok(CONV_WIDTH),
                   headmajor(DIFF_HEADS, 2 * DIFF_HALF), tok(2 * DIFF_HALF), tok(DIFF_V),
                   tok(2 * DIFF_HALF), tok(DIFF_V)],
        out_shape=[sds((MLA_HEADS, m, KV_LORA), BF16), sds((MLA_HEADS, m, ROPE_PAD), BF16),
                   sds((m, KV_LORA), F32), sds((m, MLA_ROPE), F32),
                   sds((m, KV_LORA), BF16), sds((m, ROPE_PAD), BF16),
                   sds((m, CONV_WIDTH), F32), sds((m, CONV_WIDTH), F32),
                   sds((DIFF_HEADS, m, 2 * DIFF_HALF), BF16),
                   sds((m, 2 * DIFF_HALF), F32), sds((m, DIFF_V), F32),
                   sds((m, 2 * DIFF_HALF), BF16), sds((m, DIFF_V), BF16)],
        compiler_params=_params("arbitrary"),
    )(h, pre_g, win, qn_g, kvn_g, wqn, wqr, wqs, wuk, tab_c, tab_s)


def _conv_body(xp_ref, cb_ref, cw_ref, o_ref):
    t = cb_ref.shape[1]
    conv = (cw_ref[0:1, :] * xp_ref[0, 6:6 + t, :]
            + cw_ref[1:2, :] * xp_ref[0, 7:7 + t, :]
            + cw_ref[2:3, :] * xp_ref[0, 8:8 + t, :])
    o_ref[0] = (cb_ref[0] * conv).astype(BF16)


def _conv(xpad, cb, cw):
    b, t, c = cb.shape
    return pl.pallas_call(
        _conv_body,
        grid=(b, c // LANES),
        in_specs=[pl.BlockSpec((1, t + 8, LANES), lambda i, j: (i, 0, j)),
                  pl.BlockSpec((1, t, LANES), lambda i, j: (i, 0, j)),
                  pl.BlockSpec((CONV_K, LANES), lambda i, j: (0, j))],
        out_specs=pl.BlockSpec((1, t, LANES), lambda i, j: (i, 0, j)),
        out_shape=jax.ShapeDtypeStruct((b, t, c), BF16),
        compiler_params=_params("arbitrary", "arbitrary"),
    )(xpad, cb, cw)


def _online_update(s, v, m_s, l_s, acc_s):
    m_prev = m_s[...]
    m_new = jnp.maximum(m_prev, jnp.max(s, axis=1, keepdims=True))
    alpha = jnp.exp(m_prev - m_new)
    p = jnp.exp(s - m_new)
    l_s[...] = alpha * l_s[...] + jnp.sum(p, axis=1, keepdims=True)
    acc_s[...] = alpha * acc_s[...] + _dot(p.astype(BF16), v)
    m_s[...] = m_new


def _causal_chunks(i, tq, tk, chunk):
    nfull = (i * tq + 1) // tk
    nneed = ((i + 1) * tq + tk - 1) // tk

    def full_body(j, c):
        chunk(j, False)
        return c

    def diag_body(j, c):
        chunk(j, True)
        return c

    lax.fori_loop(0, nfull, full_body, 0)
    lax.fori_loop(nfull, nneed, diag_body, 0)


def _pmla_body(ql_ref, qr_ref, ckv_ref, kr_ref, o_ref, m_s, l_s, acc_s, *, tq, tk):
    i = pl.program_id(1)
    rows = MLA_HEADS * tq
    q = ql_ref[...].reshape(rows, KV_LORA)
    qr = qr_ref[...].reshape(rows, ROPE_PAD)
    m_s[...] = jnp.full_like(m_s, NEG_INF)
    l_s[...] = jnp.zeros_like(l_s)
    acc_s[...] = jnp.zeros_like(acc_s)

    def chunk(j, masked):
        k0 = pl.multiple_of(j * tk, tk)
        kc = ckv_ref[pl.ds(k0, tk), :]
        s = _dot_nt(q, kc) + _dot_nt(qr, kr_ref[pl.ds(k0, tk), :])
        if masked:
            t = lax.broadcasted_iota(jnp.int32, (rows, tk), 0) % tq
            col = lax.broadcasted_iota(jnp.int32, (rows, tk), 1)
            s = jnp.where(col + k0 <= t + i * tq, s, NEG_INF)
        _online_update(s, kc, m_s, l_s, acc_s)

    _causal_chunks(i, tq, tk, chunk)
    o = acc_s[...] / l_s[...]
    o_ref[...] = o.astype(BF16).reshape(MLA_HEADS, tq, KV_LORA)


def _prompt_mla(ql, qr, ckvb, krb, nb, seq, tq, tk):
    nq = seq // tq
    rows = MLA_HEADS * tq
    return pl.pallas_call(
        functools.partial(_pmla_body, tq=tq, tk=tk),
        grid=(nb, nq),
        in_specs=[pl.BlockSpec((MLA_HEADS, tq, KV_LORA), lambda b, i: (0, b * nq + i, 0)),
                  pl.BlockSpec((MLA_HEADS, tq, ROPE_PAD), lambda b, i: (0, b * nq + i, 0)),
                  pl.BlockSpec((seq, KV_LORA), lambda b, i: (b, 0)),
                  pl.BlockSpec((seq, ROPE_PAD), lambda b, i: (b, 0))],
        out_specs=pl.BlockSpec((MLA_HEADS, tq, KV_LORA), lambda b, i: (0, b * nq + i, 0)),
        out_shape=jax.ShapeDtypeStruct((MLA_HEADS, nb * seq, KV_LORA), BF16),
        scratch_shapes=[pltpu.VMEM((rows, 1), F32), pltpu.VMEM((rows, 1), F32),
                        pltpu.VMEM((rows, KV_LORA), F32)],
        compiler_params=_params("arbitrary", "arbitrary"),
    )(ql, qr, ckvb, krb)


def _head_slope(h):
    return jnp.where(h == 0, 2.0 ** -2, jnp.where(h == 1, 2.0 ** -4,
                                                   jnp.where(h == 2, 2.0 ** -6, 2.0 ** -8))).astype(F32)


def _pdiff_body(dq_ref, k_ref, v_ref, lq1_ref, lk1_ref, lq2_ref, lk2_ref, dn_ref, o_ref,
                m_s, l_s, acc_s, b0_s, *, tq, tk, lam_init):
    i = pl.program_id(1)
    half = DIFF_HEADS * tq
    rows = 2 * half

    @pl.when((pl.program_id(0) == 0) & (i == 0))
    def _():
        r = lax.broadcasted_iota(jnp.int32, (rows, tk), 0)
        col = lax.broadcasted_iota(jnp.int32, (rows, tk), 1)
        b0_s[...] = _head_slope((r // tq) % DIFF_HEADS) * (col - r % tq).astype(F32)

    x = dq_ref[...].reshape(half, 2 * DIFF_HALF)
    lane = lax.broadcasted_iota(jnp.int32, x.shape, 1)
    zero = jnp.zeros_like(x)
    q = jnp.concatenate([jnp.where(lane < DIFF_HALF, x, zero),
                         jnp.where(lane >= DIFF_HALF, x, zero)], axis=0)
    slope = _head_slope((lax.broadcasted_iota(jnp.int32, (rows, 1), 0) // tq) % DIFF_HEADS)
    m_s[...] = jnp.full_like(m_s, NEG_INF)
    l_s[...] = jnp.zeros_like(l_s)
    acc_s[...] = jnp.zeros_like(acc_s)

    def chunk(j, masked):
        k0 = pl.multiple_of(j * tk, tk)
        s = _dot_nt(q, k_ref[pl.ds(k0, tk), :]) + b0_s[...] + slope * (k0 - i * tq).astype(F32)
        if masked:
            t = lax.broadcasted_iota(jnp.int32, (rows, tk), 0) % tq
            col = lax.broadcasted_iota(jnp.int32, (rows, tk), 1)
            s = jnp.where(col + k0 <= t + i * tq, s, NEG_INF)
        _online_update(s, v_ref[pl.ds(k0, tk), :], m_s, l_s, acc_s)

    _causal_chunks(i, tq, tk, chunk)
    lam = _lam(lq1_ref[...], lk1_ref[...], lq2_ref[...], lk2_ref[...], lam_init)
    o = acc_s[...] / l_s[...]
    od = _rms(o[:half] - lam * o[half:], dn_ref[...]) * (1.0 - lam_init)
    for hh in range(DIFF_HEADS):
        o_ref[:, hh * DIFF_V:(hh + 1) * DIFF_V] = od[hh * tq:(hh + 1) * tq].astype(BF16)


def _prompt_diff(dq, dkb, dvb, lq1, lk1, lq2, lk2, dn, lam_init, nb, seq, tq, tk):
    nq = seq // tq
    rows = 2 * DIFF_HEADS * tq
    vec = lambda w: pl.BlockSpec((1, w), lambda b, i: (0, 0))
    return pl.pallas_call(
        functools.partial(_pdiff_body, tq=tq, tk=tk, lam_init=lam_init),
        grid=(nb, nq),
        in_specs=[pl.BlockSpec((DIFF_HEADS, tq, 2 * DIFF_HALF), lambda b, i: (0, b * nq + i, 0)),
                  pl.BlockSpec((seq, 2 * DIFF_HALF), lambda b, i: (b, 0)),
                  pl.BlockSpec((seq, DIFF_V), lambda b, i: (b, 0)),
                  vec(DIFF_HALF), vec(DIFF_HALF), vec(DIFF_HALF), vec(DIFF_HALF), vec(DIFF_V)],
        out_specs=pl.BlockSpec((tq, DIFF_HEADS * DIFF_V), lambda b, i: (b * nq + i, 0)),
        out_shape=jax.ShapeDtypeStruct((nb * seq, DIFF_HEADS * DIFF_V), BF16),
        scratch_shapes=[pltpu.VMEM((rows, 1), F32), pltpu.VMEM((rows, 1), F32),
                        pltpu.VMEM((rows, DIFF_V), F32), pltpu.VMEM((rows, tk), F32)],
        compiler_params=_params("arbitrary", "arbitrary"),
    )(dq, dkb, dvb, lq1, lk1, lq2, lk2, dn)


_SM_ROWS = MLA_HEADS * 8
_WQ_KR = KV_LORA
_WQ_DK = KV_LORA + ROPE_PAD
_WQ_END = _WQ_DK + 2 * DIFF_HALF
NEW_PAD = 16


def _lane_slope(lane):
    return jnp.where(lane >= _SM_ROWS, _head_slope(((lane - _SM_ROWS) // 8) % DIFF_HEADS), 0.0)


def _to_col(row):
    r = lax.broadcasted_iota(jnp.int32, (LANES, LANES), 0)
    c = lax.broadcasted_iota(jnp.int32, (LANES, LANES), 1)
    return jnp.sum(jnp.where(r == c, jnp.broadcast_to(row, (LANES, LANES)), 0.0), axis=1, keepdims=True)


def _samp_body(pt_ref, wq_ref, nck_ref, nkr_ref, ndk_ref, ndv_ref,
               lq1_ref, lk1_ref, lq2_ref, lk2_ref, dn_ref,
               cckv, ckr, cdk, cdv, om_ref, od_ref,
               bckv, bkr, bdk, bdv, sems, m_s, l_s, accm_s, accd_s, b0_s,
               *, li, ppc, nc, page, past, lam_init):
    s_id = pl.program_id(0)
    c_id = pl.program_id(1)
    ns = pl.num_programs(0)
    g = s_id * nc + c_id
    slot = g % 2
    ck = ppc * page
    caches = (cckv, ckr, cdk, cdv)
    bufs = (bckv, bkr, bdk, bdv)

    def copies(s, c, sl, from_table):
        out = []
        for p in range(ppc):
            pg = pt_ref[s, c * ppc + p] if from_table else 0
            for a in range(4):
                out.append(pltpu.make_async_copy(caches[a].at[li, pg], bufs[a].at[sl, p], sems.at[sl, a]))
        return out

    @pl.when(g == 0)
    def _():
        for cp in copies(s_id, c_id, slot, True):
            cp.start()
        key = lax.broadcasted_iota(jnp.int32, (ck, LANES), 0)
        lane = lax.broadcasted_iota(jnp.int32, (ck, LANES), 1)
        b0_s[...] = _lane_slope(lane) * key.astype(F32)

    @pl.when(g + 1 < ns * nc)
    def _():
        last = c_id == nc - 1
        s_n = jnp.where(last, s_id + 1, s_id)
        c_n = jnp.where(last, 0, c_id + 1)
        for cp in copies(s_n, c_n, 1 - slot, True):
            cp.start()

    for cp in copies(0, 0, slot, False):
        cp.wait()

    @pl.when(c_id == 0)
    def _():
        m_s[...] = jnp.full_like(m_s, NEG_INF)
        l_s[...] = jnp.zeros_like(l_s)
        accm_s[...] = jnp.zeros_like(accm_s)
        accd_s[...] = jnp.zeros_like(accd_s)

    lane1 = lax.broadcasted_iota(jnp.int32, (1, LANES), 1)
    slope1 = _lane_slope(lane1)
    tok1 = lane1 % 8

    def update(st, kv, vd):
        m_prev = m_s[...]
        m_new = jnp.maximum(m_prev, jnp.max(st, axis=0, keepdims=True))
        alpha = jnp.exp(m_prev - m_new)
        p = jnp.exp(st - m_new)
        l_s[...] = alpha * l_s[...] + jnp.sum(p, axis=0, keepdims=True)
        m_s[...] = m_new
        pb = p.astype(BF16)
        a_col = _to_col(alpha)
        accm_s[...] = a_col * accm_s[...] + _dot_tn(pb, kv)
        accd_s[...] = a_col * accd_s[...] + _dot_tn(pb, vd)

    kc = bckv[slot].reshape(ck, KV_LORA).astype(BF16)
    krc = bkr[slot].reshape(ck, MLA_ROPE).astype(BF16)
    kdc = bdk[slot].reshape(ck, 2 * DIFF_HALF).astype(BF16)
    vdc = bdv[slot].reshape(ck, DIFF_V).astype(BF16)
    st = (_dot(kc, wq_ref[0, 0:_WQ_KR, :]) + _dot(krc, wq_ref[0, _WQ_KR:_WQ_KR + MLA_ROPE, :])
          + _dot(kdc, wq_ref[0, _WQ_DK:_WQ_END, :]))
    st = st + b0_s[...] + slope1 * (c_id * ck - past - tok1).astype(F32)
    update(st, kc, vdc)

    @pl.when(c_id == nc - 1)
    def _():
        nck = nck_ref[0]
        ndv = ndv_ref[0]
        sn = (_dot(nck, wq_ref[0, 0:_WQ_KR, :]) + _dot(nkr_ref[0], wq_ref[0, _WQ_KR:_WQ_DK, :])
              + _dot(ndk_ref[0], wq_ref[0, _WQ_DK:_WQ_END, :]))
        j = lax.broadcasted_iota(jnp.int32, (NEW_PAD, LANES), 0)
        lane = lax.broadcasted_iota(jnp.int32, (NEW_PAD, LANES), 1)
        tok = lane % 8
        sn = sn + _lane_slope(lane) * (j - tok).astype(F32)
        sn = jnp.where(j <= tok, sn, NEG_INF)
        update(sn, nck, ndv)
        l_col = _to_col(l_s[...])
        om_ref[0] = accm_s[0:_SM_ROWS, :] / l_col[0:_SM_ROWS]
        od_all = accd_s[...] / l_col
        hd = DIFF_HEADS * 8
        lam = _lam(lq1_ref[...], lk1_ref[...], lq2_ref[...], lk2_ref[...], lam_init)
        od = od_all[_SM_ROWS:_SM_ROWS + hd] - lam * od_all[_SM_ROWS + hd:_SM_ROWS + 2 * hd]
        od_ref[0] = _rms(od, dn_ref[...]) * (1.0 - lam_init)


def _sample_attn(page_table, wq, nck, nkr, ndk, ndv, lq1, lk1, lq2, lk2, dn,
                 cache_ckv, cache_krope, cache_dk, cache_dv, li, lam_init, ppc):
    ns, npages = page_table.shape
    page = cache_ckv.shape[2]
    nc = npages // ppc
    ck = ppc * page
    hd = DIFF_HEADS * 8
    per_seq = lambda shape: pl.BlockSpec((1,) + shape, lambda s, c, pt: (s, 0, 0))
    vec = lambda w: pl.BlockSpec((1, w), lambda s, c, pt: (0, 0))
    hbm = pl.BlockSpec(memory_space=pl.ANY)
    grid_spec = pltpu.PrefetchScalarGridSpec(
        num_scalar_prefetch=1,
        grid=(ns, nc),
        in_specs=[per_seq((_WQ_END, LANES)), per_seq((NEW_PAD, KV_LORA)), per_seq((NEW_PAD, ROPE_PAD)),
                  per_seq((NEW_PAD, 2 * DIFF_HALF)), per_seq((NEW_PAD, DIFF_V)),
                  vec(DIFF_HALF), vec(DIFF_HALF), vec(DIFF_HALF), vec(DIFF_HALF), vec(DIFF_V),
                  hbm, hbm, hbm, hbm],
        out_specs=[per_seq((_SM_ROWS, KV_LORA)), per_seq((hd, DIFF_V))],
        scratch_shapes=[pltpu.VMEM((2, ppc, page, KV_LORA), F32),
                        pltpu.VMEM((2, ppc, page, MLA_ROPE), F32),
                        pltpu.VMEM((2, ppc, page, 2 * DIFF_HALF), F32),
                        pltpu.VMEM((2, ppc, page, DIFF_V), F32),
                        pltpu.SemaphoreType.DMA((2, 4)),
                        pltpu.VMEM((1, LANES), F32), pltpu.VMEM((1, LANES), F32),
                        pltpu.VMEM((LANES, KV_LORA), F32), pltpu.VMEM((LANES, DIFF_V), F32),
                        pltpu.VMEM((ck, LANES), F32)],
    )
    return pl.pallas_call(
        functools.partial(_samp_body, li=li, ppc=ppc, nc=nc, page=page, past=npages * page,
                          lam_init=lam_init),
        grid_spec=grid_spec,
        out_shape=[jax.ShapeDtypeStruct((ns, _SM_ROWS, KV_LORA), F32),
                   jax.ShapeDtypeStruct((ns, hd, DIFF_V), F32)],
        compiler_params=_params("arbitrary", "arbitrary"),
    )(page_table, wq, nck, nkr, ndk, ndv, lq1, lk1, lq2, lk2, dn,
      cache_ckv, cache_krope, cache_dk, cache_dv)


def _mixout_body(h_ref, ol_ref, conv_ref, diff_ref, wuv_ref, wo_ref, post_ref, o_ref):
    parts = [_dot(ol_ref[hh], wuv_ref[hh]).astype(BF16) for hh in range(MLA_HEADS)]
    mla = jnp.concatenate(parts, axis=1)
    n_mla = mla.shape[1]
    n_conv = n_mla + CONV_WIDTH
    mixed = (_dot(mla, wo_ref[0:n_mla, :]) + _dot(conv_ref[...], wo_ref[n_mla:n_conv, :])
             + _dot(diff_ref[...], wo_ref[n_conv:, :]))
    o_ref[...] = h_ref[...] + _rms(mixed, post_ref[...])


def _mixout(h, o_lat, conv_o, diff_o, wuv, wo, post_g, tm):
    m, d = h.shape
    tok = lambda w: pl.BlockSpec((tm, w), lambda i: (i, 0))
    return pl.pallas_call(
        _mixout_body,
        grid=(m // tm,),
        in_specs=[tok(d), pl.BlockSpec((MLA_HEADS, tm, KV_LORA), lambda i: (0, i, 0)),
                  tok(CONV_WIDTH), tok(DIFF_HEADS * DIFF_V),
                  _const_spec(wuv.shape), _const_spec(wo.shape), _const_spec((1, d))],
        out_specs=tok(d),
        out_shape=jax.ShapeDtypeStruct((m, d), F32),
        compiler_params=_params("arbitrary"),
    )(h, o_lat, conv_o, diff_o, wuv, wo, post_g)


def _swap_halves(w):
    half = w.shape[-1] // 2
    return jnp.concatenate([w[..., half:], w[..., :half]], axis=-1)


def _pad_last(w, width):
    return jnp.pad(w, [(0, 0)] * (w.ndim - 1) + [(0, width - w.shape[-1])])


def _pack_w_in(w):
    s = np.cumsum([0, Q_LORA, KV_LORA, MLA_ROPE, CONV_WIDTH, CONV_WIDTH, CONV_WIDTH,
                   DIFF_HEADS * 2 * DIFF_HALF, 2 * DIFF_HALF, DIFF_V])
    cq, ckv, kr, cb, cc, cx, dq, dk, dv = [w[:, s[k]:s[k + 1]] for k in range(9)]
    return jnp.concatenate([cq, ckv, _pad_last(kr, ROPE_PAD), _pad_last(_swap_halves(kr), ROPE_PAD),
                            cb, cc, cx, dq, dk, dv], axis=1).astype(BF16)


def _rope_tables(pos):
    half = MLA_ROPE // 2
    freqs = ROPE_THETA ** (-jnp.arange(half, dtype=F32) / half)
    ang = pos.astype(F32)[:, None] * freqs[None, :]
    cos, sin = jnp.cos(ang), jnp.sin(ang)
    return (jnp.concatenate([cos, cos, cos, cos], axis=1),
            jnp.concatenate([-sin, sin, -sin, sin], axis=1))


def _tile(n, pref):
    for t in pref:
        if n % t == 0:
            return t
    return n


def kernel(x_prompt, x_sample, cache_ckv, cache_krope, cache_diff_k, cache_diff_v, state_conv, page_table, ffn1_pre_g, ffn1_post_g, ffn1_w_gate, ffn1_w_up, ffn1_w_down, mix_pre_g, mix_post_g, w_in, q_norm_g, kv_norm_g, w_uq, w_uk, w_uv, conv_w, diff_lq1, diff_lk1, diff_lq2, diff_lk2, diff_norm_g, w_o, ffn2_pre_g, ffn2_post_g, ffn2_w_gate, ffn2_w_up, ffn2_w_down):
    nb, seq, d = x_prompt.shape
    ns, dseq, _ = x_sample.shape
    depth = w_in.shape[0]
    npages, page = page_table.shape[1], cache_ckv.shape[2]
    past = npages * page
    mp, msamp = nb * seq, ns * dseq
    m = mp + msamp
    assert dseq == 8 and MLA_HEADS * dseq == _SM_ROWS

    tm_ffn = _tile(m, (512, 256, 128))
    tf = _tile(ffn1_w_gate.shape[2], (512, 256, 128))
    tm_mix = _tile(m, (256, 128))
    tq = _tile(seq, (256, 128))
    tk = _tile(seq, (512, 256, 128))
    ppc = _tile(npages, (16, 8, 4, 2, 1))

    pos = jnp.concatenate([jnp.tile(jnp.arange(seq, dtype=jnp.int32), nb),
                           jnp.tile(past + jnp.arange(dseq, dtype=jnp.int32), ns)])
    tab_c, tab_s = _rope_tables(pos)

    h = jnp.concatenate([x_prompt.reshape(mp, d), x_sample.reshape(msamp, d)], axis=0)
    row = lambda v: v.reshape(1, -1)
    states = []
    for li in range(depth):
        lam_init = 0.8 - 0.6 * math.exp(-0.3 * li)
        uq = w_uq[li]
        wqn = uq[:, :, :MLA_NOPE].reshape(Q_LORA, MLA_HEADS * MLA_NOPE).astype(BF16)
        uqr = uq[:, :, MLA_NOPE:]
        wqr = _pad_last(uqr, ROPE_PAD).reshape(Q_LORA, MLA_HEADS * ROPE_PAD).astype(BF16)
        wqs = _pad_last(_swap_halves(uqr), ROPE_PAD).reshape(Q_LORA, MLA_HEADS * ROPE_PAD).astype(BF16)
        wuk = jnp.transpose(w_uk[li], (1, 2, 0)).astype(BF16)
        wuv = jnp.transpose(w_uv[li], (1, 0, 2)).astype(BF16)
        win = _pack_w_in(w_in[li])
        lq1, lk1, lq2, lk2 = row(diff_lq1[li]), row(diff_lk1[li]), row(diff_lq2[li]), row(diff_lk2[li])
        dn = row(diff_norm_g[li])

        h = _ffn(h, row(ffn1_pre_g[li]), ffn1_w_gate[li].astype(BF16), ffn1_w_up[li].astype(BF16),
                 ffn1_w_down[li].astype(BF16), row(ffn1_post_g[li]), tm_ffn, tf)

        (ql, qr, ckv, kr, ckvb, krb, cb, u, dq, dk, dv, dkb, dvb) = _mixin(
            h, row(mix_pre_g[li]), win, row(q_norm_g[li]), row(kv_norm_g[li]),
            wqn, wqr, wqs, wuk, tab_c, tab_s, tm_mix)

        cw = conv_w[li]
        u_p = u[:mp].reshape(nb, seq, CONV_WIDTH)
        xpad_p = jnp.concatenate([jnp.zeros((nb, 8, CONV_WIDTH), F32), u_p], axis=1)
        conv_p = _conv(xpad_p, cb[:mp].reshape(nb, seq, CONV_WIDTH), cw)
        u_s = u[mp:].reshape(ns, dseq, CONV_WIDTH)
        xs = jnp.concatenate([jnp.zeros((ns, 8 - (CONV_K - 1), CONV_WIDTH), F32), state_conv[li], u_s], axis=1)
        xpad_s = jnp.concatenate([jnp.zeros((1, 8, CONV_WIDTH), F32), xs.reshape(1, ns * 16, CONV_WIDTH)], axis=1)
        cb_s = jnp.concatenate([jnp.zeros((ns, 8, CONV_WIDTH), F32), cb[mp:].reshape(ns, dseq, CONV_WIDTH)], axis=1)
        conv_s = _conv(xpad_s, cb_s.reshape(1, ns * 16, CONV_WIDTH), cw)
        conv_s = conv_s.reshape(ns, 16, CONV_WIDTH)[:, 8:, :]
        conv_o = jnp.concatenate([conv_p.reshape(mp, CONV_WIDTH), conv_s.reshape(msamp, CONV_WIDTH)], axis=0)
        conv_state_p = u_p[:, seq - (CONV_K - 1):, :]
        conv_state_s = xs[:, 16 - (CONV_K - 1):, :]

        ol_p = _prompt_mla(ql, qr, ckvb, krb, nb, seq, tq, tk)
        diff_p = _prompt_diff(dq, dkb, dvb, lq1, lk1, lq2, lk2, dn, lam_init, nb, seq, tq, tk)

        def per_seq_t(a, nh):
            w = a.shape[-1]
            a = a[:, mp:, :].reshape(nh, ns, dseq, w)
            return jnp.transpose(a, (1, 3, 0, 2)).reshape(ns, w, nh * dseq)

        dq_t = per_seq_t(dq, DIFF_HEADS)
        lane_lo = (jnp.arange(2 * DIFF_HALF) < DIFF_HALF)[None, :, None]
        zq = jnp.zeros_like(dq_t)
        wq_m = jnp.concatenate([per_seq_t(ql, MLA_HEADS), per_seq_t(qr, MLA_HEADS)], axis=1)
        wq_m = jnp.concatenate([wq_m, jnp.zeros((ns, _WQ_DK, LANES - _SM_ROWS), BF16)], axis=2)
        wq_d = jnp.concatenate([jnp.zeros((ns, 2 * DIFF_HALF, _SM_ROWS), BF16),
                                jnp.where(lane_lo, dq_t, zq), jnp.where(lane_lo, zq, dq_t)], axis=2)
        wq = jnp.concatenate([wq_m, wq_d], axis=1)

        def new_rows(a):
            a = a[mp:].reshape(ns, dseq, a.shape[-1])
            return jnp.pad(a, ((0, 0), (0, NEW_PAD - dseq), (0, 0)))

        om, od = _sample_attn(page_table, wq, new_rows(ckvb), new_rows(krb), new_rows(dkb), new_rows(dvb),
                              lq1, lk1, lq2, lk2, dn, cache_ckv, cache_krope, cache_diff_k, cache_diff_v,
                              li, lam_init, ppc)
        ol_s = jnp.transpose(om.reshape(ns, MLA_HEADS, dseq, KV_LORA), (1, 0, 2, 3))
        ol = jnp.concatenate([ol_p, ol_s.reshape(MLA_HEADS, msamp, KV_LORA).astype(BF16)], axis=1)
        diff_s = jnp.transpose(od.reshape(ns, DIFF_HEADS, dseq, DIFF_V), (0, 2, 1, 3))
        diff_o = jnp.concatenate([diff_p, diff_s.reshape(msamp, DIFF_HEADS * DIFF_V).astype(BF16)], axis=0)

        h = _mixout(h, ol, conv_o, diff_o, wuv, w_o[li].astype(BF16), row(mix_post_g[li]), tm_mix)
        h = _ffn(h, row(ffn2_pre_g[li]), ffn2_w_gate[li].astype(BF16), ffn2_w_up[li].astype(BF16),
                 ffn2_w_down[li].astype(BF16), row(ffn2_post_g[li]), tm_ffn, tf)
        states.append((ckv, kr, dk, dv, conv_state_p, conv_state_s))

    def stack(k, lo, hi, shape):
        return jnp.stack([s[k][lo:hi].reshape(shape + (s[k].shape[-1],)) for s in states])

    p_shape, s_shape = (nb, seq), (ns, dseq)
    return (h[:mp].reshape(nb, seq, d), h[mp:].reshape(ns, dseq, d),
            stack(0, 0, mp, p_shape), stack(1, 0, mp, p_shape), stack(2, 0, mp, p_shape), stack(3, 0, mp, p_shape),
            jnp.stack([s[4] for s in states]),
            stack(0, mp, m, s_shape), stack(1, mp, m, s_shape), stack(2, mp, m, s_shape), stack(3, mp, m, s_shape),
            jnp.stack([s[5] for s in states]))
```

```python
import functools
import math

import numpy as np
import jax
import jax.numpy as jnp
from jax import lax
from jax.experimental import pallas as pl
from jax.experimental.pallas import tpu as pltpu

F32 = jnp.float32
BF16 = jnp.bfloat16

MLA_HEADS = 8
MLA_NOPE = 128
MLA_ROPE = 64
KV_LORA = 256
Q_LORA = 512
CONV_WIDTH = 512
CONV_K = 3
DIFF_HEADS = 4
DIFF_HALF = 64
DIFF_V = 2 * DIFF_HALF
ROPE_THETA = 10000.0
NORM_EPS = 1e-6
NEG_INF = -1e30
LANES = 128
ROPE_PAD = LANES
VMEM_LIMIT = 56 * 1024 * 1024

MLA_SCALE = (MLA_NOPE + MLA_ROPE) ** -0.5
DIFF_SCALE = DIFF_HALF ** -0.5

_C_CQ = 0
_C_CKV = _C_CQ + Q_LORA
_C_KR = _C_CKV + KV_LORA
_C_KRS = _C_KR + ROPE_PAD
_C_CB = _C_KRS + ROPE_PAD
_C_CC = _C_CB + CONV_WIDTH
_C_CX = _C_CC + CONV_WIDTH
_C_DQ = _C_CX + CONV_WIDTH
_C_DK = _C_DQ + DIFF_HEADS * 2 * DIFF_HALF
_C_DV = _C_DK + 2 * DIFF_HALF
_C_END = _C_DV + DIFF_V


def _params(*sem):
    return pltpu.CompilerParams(dimension_semantics=sem, vmem_limit_bytes=VMEM_LIMIT)


def _rms(x, g):
    ms = jnp.mean(x * x, axis=-1, keepdims=True)
    return x * lax.rsqrt(ms + NORM_EPS) * g


def _dot(a, b):
    return jnp.dot(a, b, preferred_element_type=F32)


def _dot_nt(a, b):
    return lax.dot_general(a, b, (((1,), (1,)), ((), ())), preferred_element_type=F32)


def _dot_tn(a, b):
    return lax.dot_general(a, b, (((0,), (0,)), ((), ())), preferred_element_type=F32)


def _lam(lq1, lk1, lq2, lk2, lam_init):
    a = jnp.sum(lq1 * lk1, axis=-1, keepdims=True)
    b = jnp.sum(lq2 * lk2, axis=-1, keepdims=True)
    return jnp.exp(a) - jnp.exp(b) + lam_init


def _const_spec(shape):
    nd = len(shape)
    return pl.BlockSpec(shape, lambda *_: (0,) * nd, pipeline_mode=pl.Buffered(1))


def _ffn_body(x_ref, pre_ref, wg_ref, wu_ref, wd_ref, post_ref, o_ref, xn_ref, *, nf):
    f = pl.program_id(1)

    @pl.when(f == 0)
    def _():
        xn_ref[...] = _rms(x_ref[...], pre_ref[...]).astype(BF16)
        o_ref[...] = jnp.zeros_like(o_ref)

    xn = xn_ref[...]
    g = _dot(xn, wg_ref[...])
    u = _dot(xn, wu_ref[...])
    a = (g * (1.0 / (1.0 + jnp.exp(-g)))) * u
    o_ref[...] += _dot(a.astype(BF16), wd_ref[...])

    @pl.when(f == nf - 1)
    def _():
        o_ref[...] = x_ref[...] + 0.5 * _rms(o_ref[...], post_ref[...])


def _ffn(x, pre_g, wg, wu, wd, post_g, tm, tf):
    m, d = x.shape
    nf = wg.shape[1] // tf
    return pl.pallas_call(
        functools.partial(_ffn_body, nf=nf),
        grid=(m // tm, nf),
        in_specs=[
            pl.BlockSpec((tm, d), lambda i, f: (i, 0)),
            pl.BlockSpec((1, d), lambda i, f: (0, 0)),
            pl.BlockSpec((d, tf), lambda i, f: (0, f)),
            pl.BlockSpec((d, tf), lambda i, f: (0, f)),
            pl.BlockSpec((tf, d), lambda i, f: (f, 0)),
            pl.BlockSpec((1, d), lambda i, f: (0, 0)),
        ],
        out_specs=pl.BlockSpec((tm, d), lambda i, f: (i, 0)),
        out_shape=jax.ShapeDtypeStruct((m, d), F32),
        scratch_shapes=[pltpu.VMEM((tm, d), BF16)],
        compiler_params=_params("arbitrary", "arbitrary"),
    )(x, pre_g, wg, wu, wd, post_g)


def _mixin_body(h_ref, pre_ref, win_ref, qn_ref, kvn_ref, wqn_ref, wqr_ref, wqs_ref, wuk_ref,
                tc_ref, ts_ref,
                ql_ref, qr_ref, ckv_ref, kr_ref, ckvb_ref, krb_ref, cb_ref, u_ref,
                dq_ref, dk_ref, dv_ref, dkb_ref, dvb_ref):
    n = _rms(h_ref[...], pre_ref[...]).astype(BF16)
    tab_c = tc_ref[...]
    tab_s = ts_ref[...]

    def proj(c0, c1):
        return _dot(n, win_ref[:, c0:c1])

    cq = _rms(proj(_C_CQ, _C_CKV), qn_ref[...]).astype(BF16)
    qn = _dot(cq, wqn_ref[...]).astype(BF16)
    qr = _dot(cq, wqr_ref[...])
    qs = _dot(cq, wqs_ref[...])
    for hh in range(MLA_HEADS):
        q_lat = _dot(qn[:, hh * MLA_NOPE:(hh + 1) * MLA_NOPE], wuk_ref[hh])
        ql_ref[hh] = (q_lat * MLA_SCALE).astype(BF16)
        sl = slice(hh * ROPE_PAD, (hh + 1) * ROPE_PAD)
        qr_ref[hh] = ((qr[:, sl] * tab_c + qs[:, sl] * tab_s) * MLA_SCALE).astype(BF16)
    ckv = _rms(proj(_C_CKV, _C_KR), kvn_ref[...])
    ckv_ref[...] = ckv
    ckvb_ref[...] = ckv.astype(BF16)
    kr = proj(_C_KR, _C_KRS) * tab_c + proj(_C_KRS, _C_CB) * tab_s
    kr_ref[...] = kr[:, :MLA_ROPE]
    krb_ref[...] = kr.astype(BF16)
    cb_ref[...] = proj(_C_CB, _C_CC)
    u_ref[...] = proj(_C_CC, _C_CX) * proj(_C_CX, _C_DQ)
    dq = proj(_C_DQ, _C_DK) * DIFF_SCALE
    for hh in range(DIFF_HEADS):
        dq_ref[hh] = dq[:, hh * 2 * DIFF_HALF:(hh + 1) * 2 * DIFF_HALF].astype(BF16)
    dk = proj(_C_DK, _C_DV)
    dv = proj(_C_DV, _C_END)
    dk_ref[...] = dk
    dv_ref[...] = dv
    dkb_ref[...] = dk.astype(BF16)
    dvb_ref[...] = dv.astype(BF16)


def _mixin(h, pre_g, win, qn_g, kvn_g, wqn, wqr, wqs, wuk, tab_c, tab_s, tm):
    m, d = h.shape
    tok = lambda w: pl.BlockSpec((tm, w), lambda i: (i, 0))
    headmajor = lambda nh, w: pl.BlockSpec((nh, tm, w), lambda i: (0, i, 0))
    sds = jax.ShapeDtypeStruct
    return pl.pallas_call(
        _mixin_body,
        grid=(m // tm,),
        in_specs=[tok(d), _const_spec((1, d)), _const_spec(win.shape), _const_spec((1, Q_LORA)),
                  _const_spec((1, KV_LORA)), _const_spec(wqn.shape), _const_spec(wqr.shape),
                  _const_spec(wqs.shape), _const_spec(wuk.shape), tok(LANES), tok(LANES)],
        out_specs=[headmajor(MLA_HEADS, KV_LORA), headmajor(MLA_HEADS, ROPE_PAD),
                   tok(KV_LORA), tok(MLA_ROPE), tok(KV_LORA), tok(ROPE_PAD),
                   tok(CONV_WIDTH), tok(CONV_WIDTH),
                   headmajor(DIFF_HEADS, 2 * DIFF_HALF), tok(2 * DIFF_HALF), tok(DIFF_V),
                   tok(2 * DIFF_HALF), tok(DIFF_V)],
        out_shape=[sds((MLA_HEADS, m, KV_LORA), BF16), sds((MLA_HEADS, m, ROPE_PAD), BF16),
                   sds((m, KV_LORA), F32), sds((m, MLA_ROPE), F32),
                   sds((m, KV_LORA), BF16), sds((m, ROPE_PAD), BF16),
                   sds((m, CONV_WIDTH), F32), sds((m, CONV_WIDTH), F32),
                   sds((DIFF_HEADS, m, 2 * DIFF_HALF), BF16),
                   sds((m, 2 * DIFF_HALF), F32), sds((m, DIFF_V), F32),
                   sds((m, 2 * DIFF_HALF), BF16), sds((m, DIFF_V), BF16)],
        compiler_params=_params("arbitrary"),
    )(h, pre_g, win, qn_g, kvn_g, wqn, wqr, wqs, wuk, tab_c, tab_s)


def _conv_body(xp_ref, cb_ref, cw_ref, o_ref):
    t = cb_ref.shape[1]
    conv = (cw_ref[0:1, :] * xp_ref[0, 6:6 + t, :]
            + cw_ref[1:2, :] * xp_ref[0, 7:7 + t, :]
            + cw_ref[2:3, :] * xp_ref[0, 8:8 + t, :])
    o_ref[0] = (cb_ref[0] * conv).astype(BF16)


def _conv(xpad, cb, cw):
    b, t, c = cb.shape
    return pl.pallas_call(
        _conv_body,
        grid=(b, c // LANES),
        in_specs=[pl.BlockSpec((1, t + 8, LANES), lambda i, j: (i, 0, j)),
                  pl.BlockSpec((1, t, LANES), lambda i, j: (i, 0, j)),
                  pl.BlockSpec((CONV_K, LANES), lambda i, j: (0, j))],
        out_specs=pl.BlockSpec((1, t, LANES), lambda i, j: (i, 0, j)),
        out_shape=jax.ShapeDtypeStruct((b, t, c), BF16),
        compiler_params=_params("arbitrary", "arbitrary"),
    )(xpad, cb, cw)


def _causal_softmax_pv(i, tq, tk, score, v_ref, s_s, mb_s, l_s, acc_s):
    rows = acc_s.shape[0]
    nblk = tk // LANES
    nfull = (i * tq + 1) // tk
    nneed = ((i + 1) * tq + tk - 1) // tk
    mb_s[...] = jnp.full_like(mb_s, NEG_INF)

    def pass1(masked):
        def body(j, c):
            k0 = pl.multiple_of(j * tk, tk)
            s = score(j, k0, masked)
            if masked:
                t = lax.broadcasted_iota(jnp.int32, (rows, tk), 0) % tq
                col = lax.broadcasted_iota(jnp.int32, (rows, tk), 1)
                s = jnp.where(col + k0 <= t + i * tq, s, NEG_INF)
            s_s[j] = s
            mr = mb_s[...]
            for b in range(nblk):
                mr = jnp.maximum(mr, s[:, b * LANES:(b + 1) * LANES])
            mb_s[...] = mr
            return c
        return body

    lax.fori_loop(0, nfull, pass1(False), 0)
    lax.fori_loop(nfull, nneed, pass1(True), 0)
    m = jnp.max(mb_s[...], axis=1, keepdims=True)
    mb_s[...] = jnp.broadcast_to(m, mb_s.shape)
    l_s[...] = jnp.zeros_like(l_s)
    acc_s[...] = jnp.zeros_like(acc_s)

    def pass2(j, c):
        k0 = pl.multiple_of(j * tk, tk)
        mb = mb_s[...]
        lr = l_s[...]
        parts = []
        for b in range(nblk):
            pb = jnp.exp(s_s[j, :, b * LANES:(b + 1) * LANES] - mb)
            lr = lr + pb
            parts.append(pb.astype(BF16))
        l_s[...] = lr
        acc_s[...] += _dot(jnp.concatenate(parts, axis=1), v_ref[pl.ds(k0, tk), :])
        return c

    lax.fori_loop(0, nneed, pass2, 0)
    return acc_s[...] / jnp.sum(l_s[...], axis=1, keepdims=True)


def _pmla_body(ql_ref, qr_ref, ckv_ref, kr_ref, o_ref, s_s, mb_s, l_s, acc_s, *, tq, tk):
    i = pl.program_id(1)
    rows = MLA_HEADS * tq
    q = ql_ref[...].reshape(rows, KV_LORA)
    qr = qr_ref[...].reshape(rows, ROPE_PAD)

    def score(j, k0, masked):
        return _dot_nt(q, ckv_ref[pl.ds(k0, tk), :]) + _dot_nt(qr, kr_ref[pl.ds(k0, tk), :])

    o = _causal_softmax_pv(i, tq, tk, score, ckv_ref, s_s, mb_s, l_s, acc_s)
    o_ref[...] = o.astype(BF16).reshape(MLA_HEADS, tq, KV_LORA)


def _prompt_mla(ql, qr, ckvb, krb, nb, seq, tq, tk):
    nq = seq // tq
    rows = MLA_HEADS * tq
    return pl.pallas_call(
        functools.partial(_pmla_body, tq=tq, tk=tk),
        grid=(nb, nq),
        in_specs=[pl.BlockSpec((MLA_HEADS, tq, KV_LORA), lambda b, i: (0, b * nq + i, 0)),
                  pl.BlockSpec((MLA_HEADS, tq, ROPE_PAD), lambda b, i: (0, b * nq + i, 0)),
                  pl.BlockSpec((seq, KV_LORA), lambda b, i: (b, 0)),
                  pl.BlockSpec((seq, ROPE_PAD), lambda b, i: (b, 0))],
        out_specs=pl.BlockSpec((MLA_HEADS, tq, KV_LORA), lambda b, i: (0, b * nq + i, 0)),
        out_shape=jax.ShapeDtypeStruct((MLA_HEADS, nb * seq, KV_LORA), BF16),
        scratch_shapes=[pltpu.VMEM((seq // tk, rows, tk), F32), pltpu.VMEM((rows, LANES), F32),
                        pltpu.VMEM((rows, LANES), F32), pltpu.VMEM((rows, KV_LORA), F32)],
        compiler_params=_params("arbitrary", "arbitrary"),
    )(ql, qr, ckvb, krb)


def _head_slope(h):
    return jnp.where(h == 0, 2.0 ** -2, jnp.where(h == 1, 2.0 ** -4,
                                                   jnp.where(h == 2, 2.0 ** -6, 2.0 ** -8))).astype(F32)


def _pdiff_body(dq_ref, k_ref, v_ref, lq1_ref, lk1_ref, lq2_ref, lk2_ref, dn_ref, o_ref,
                s_s, mb_s, l_s, acc_s, b0_s, *, tq, tk, lam_init):
    i = pl.program_id(1)
    half = DIFF_HEADS * tq
    rows = 2 * half

    @pl.when((pl.program_id(0) == 0) & (i == 0))
    def _():
        r = lax.broadcasted_iota(jnp.int32, (rows, tk), 0)
        col = lax.broadcasted_iota(jnp.int32, (rows, tk), 1)
        b0_s[...] = _head_slope((r // tq) % DIFF_HEADS) * (col - r % tq).astype(F32)

    x = dq_ref[...].reshape(half, 2 * DIFF_HALF)
    lane = lax.broadcasted_iota(jnp.int32, x.shape, 1)
    zero = jnp.zeros_like(x)
    q = jnp.concatenate([jnp.where(lane < DIFF_HALF, x, zero),
                         jnp.where(lane >= DIFF_HALF, x, zero)], axis=0)
    slope = _head_slope((lax.broadcasted_iota(jnp.int32, (rows, LANES), 0) // tq) % DIFF_HEADS)

    def score(j, k0, masked):
        off = slope * (k0 - i * tq).astype(F32)
        return (_dot_nt(q, k_ref[pl.ds(k0, tk), :]) + b0_s[...]
                + jnp.concatenate([off] * (tk // LANES), axis=1))

    o = _causal_softmax_pv(i, tq, tk, score, v_ref, s_s, mb_s, l_s, acc_s)
    lam = _lam(lq1_ref[...], lk1_ref[...], lq2_ref[...], lk2_ref[...], lam_init)
    od = _rms(o[:half] - lam * o[half:], dn_ref[...]) * (1.0 - lam_init)
    for hh in range(DIFF_HEADS):
        o_ref[:, hh * DIFF_V:(hh + 1) * DIFF_V] = od[hh * tq:(hh + 1) * tq].astype(BF16)


def _prompt_diff(dq, dkb, dvb, lq1, lk1, lq2, lk2, dn, lam_init, nb, seq, tq, tk):
    nq = seq // tq
    rows = 2 * DIFF_HEADS * tq
    vec = lambda w: pl.BlockSpec((1, w), lambda b, i: (0, 0))
    return pl.pallas_call(
        functools.partial(_pdiff_body, tq=tq, tk=tk, lam_init=lam_init),
        grid=(nb, nq),
        in_specs=[pl.BlockSpec((DIFF_HEADS, tq, 2 * DIFF_HALF), lambda b, i: (0, b * nq + i, 0)),
                  pl.BlockSpec((seq, 2 * DIFF_HALF), lambda b, i: (b, 0)),
                  pl.BlockSpec((seq, DIFF_V), lambda b, i: (b, 0)),
                  vec(DIFF_HALF), vec(DIFF_HALF), vec(DIFF_HALF), vec(DIFF_HALF), vec(DIFF_V)],
        out_specs=pl.BlockSpec((tq, DIFF_HEADS * DIFF_V), lambda b, i: (b * nq + i, 0)),
        out_shape=jax.ShapeDtypeStruct((nb * seq, DIFF_HEADS * DIFF_V), BF16),
        scratch_shapes=[pltpu.VMEM((seq // tk, rows, tk), F32), pltpu.VMEM((rows, LANES), F32),
                        pltpu.VMEM((rows, LANES), F32), pltpu.VMEM((rows, DIFF_V), F32),
                        pltpu.VMEM((rows, tk), F32)],
        compiler_params=_params("arbitrary", "arbitrary"),
    )(dq, dkb, dvb, lq1, lk1, lq2, lk2, dn)


_SM_ROWS = MLA_HEADS * 8
_WQ_KR = KV_LORA
_WQ_DK = KV_LORA + ROPE_PAD
_WQ_END = _WQ_DK + 2 * DIFF_HALF
NEW_PAD = 16


def _lane_slope(lane):
    return jnp.where(lane >= _SM_ROWS, _head_slope(((lane - _SM_ROWS) // 8) % DIFF_HEADS), 0.0)


def _to_col(row):
    r = lax.broadcasted_iota(jnp.int32, (LANES, LANES), 0)
    c = lax.broadcasted_iota(jnp.int32, (LANES, LANES), 1)
    return jnp.sum(jnp.where(r == c, jnp.broadcast_to(row, (LANES, LANES)), 0.0), axis=1, keepdims=True)


def _samp_body(pt_ref, wq_ref, nck_ref, nkr_ref, ndk_ref, ndv_ref,
               lq1_ref, lk1_ref, lq2_ref, lk2_ref, dn_ref,
               cckv, ckr, cdk, cdv, om_ref, od_ref,
               bckv, bkr, bdk, bdv, sems, m_s, l_s, accm_s, accd_s, b0_s,
               *, li, ppc, nc, page, past, lam_init):
    s_id = pl.program_id(0)
    c_id = pl.program_id(1)
    ns = pl.num_programs(0)
    g = s_id * nc + c_id
    slot = g % 2
    ck = ppc * page
    caches = (cckv, ckr, cdk, cdv)
    bufs = (bckv, bkr, bdk, bdv)

    def copies(s, c, sl, from_table):
        out = []
        for p in range(ppc):
            pg = pt_ref[s, c * ppc + p] if from_table else 0
            for a in range(4):
                dst = bkr.at[sl, :, pl.ds(p * page, page)] if a == 1 else bufs[a].at[sl, p]
                out.append(pltpu.make_async_copy(caches[a].at[li, pg], dst, sems.at[sl, a]))
        return out

    @pl.when(g == 0)
    def _():
        for cp in copies(s_id, c_id, slot, True):
            cp.start()
        key = lax.broadcasted_iota(jnp.int32, (ck, LANES), 0)
        lane = lax.broadcasted_iota(jnp.int32, (ck, LANES), 1)
        b0_s[...] = _lane_slope(lane) * key.astype(F32)

    @pl.when(g + 1 < ns * nc)
    def _():
        last = c_id == nc - 1
        s_n = jnp.where(last, s_id + 1, s_id)
        c_n = jnp.where(last, 0, c_id + 1)
        for cp in copies(s_n, c_n, 1 - slot, True):
            cp.start()

    for cp in copies(0, 0, slot, False):
        cp.wait()

    @pl.when(c_id == 0)
    def _():
        m_s[...] = jnp.full_like(m_s, NEG_INF)
        l_s[...] = jnp.zeros_like(l_s)
        accm_s[...] = jnp.zeros_like(accm_s)
        accd_s[...] = jnp.zeros_like(accd_s)

    lane1 = lax.broadcasted_iota(jnp.int32, (1, LANES), 1)
    slope1 = _lane_slope(lane1)
    tok1 = lane1 % 8

    def update(st, off, kv, vd):
        m_prev = m_s[...]
        m_new = jnp.maximum(m_prev, jnp.max(st, axis=0, keepdims=True) + off)
        alpha = jnp.exp(m_prev - m_new)
        p = jnp.exp(st - (m_new - off))
        l_s[...] = alpha * l_s[...] + jnp.sum(p, axis=0, keepdims=True)
        m_s[...] = m_new
        pb = p.astype(BF16)
        a_col = _to_col(alpha)
        accm_s[...] = a_col * accm_s[...] + _dot_tn(pb, kv)
        accd_s[...] = a_col * accd_s[...] + _dot_tn(pb, vd)

    kc = bckv[slot].reshape(ck, KV_LORA).astype(BF16)
    krt = bkr[slot].astype(BF16)
    kdc = bdk[slot].reshape(ck, 2 * DIFF_HALF).astype(BF16)
    vdc = bdv[slot].reshape(ck, DIFF_V).astype(BF16)
    st = (_dot(kc, wq_ref[0, 0:_WQ_KR, :]) + _dot_tn(krt, wq_ref[0, _WQ_KR:_WQ_KR + MLA_ROPE, :])
          + _dot(kdc, wq_ref[0, _WQ_DK:_WQ_END, :]) + b0_s[...])
    update(st, slope1 * (c_id * ck - past - tok1).astype(F32), kc, vdc)

    @pl.when(c_id == nc - 1)
    def _():
        nck = nck_ref[0]
        ndv = ndv_ref[0]
        sn = (_dot(nck, wq_ref[0, 0:_WQ_KR, :]) + _dot(nkr_ref[0], wq_ref[0, _WQ_KR:_WQ_DK, :])
              + _dot(ndk_ref[0], wq_ref[0, _WQ_DK:_WQ_END, :]))
        j = lax.broadcasted_iota(jnp.int32, (NEW_PAD, LANES), 0)
        lane = lax.broadcasted_iota(jnp.int32, (NEW_PAD, LANES), 1)
        tok = lane % 8
        sn = sn + _lane_slope(lane) * (j - tok).astype(F32)
        sn = jnp.where(j <= tok, sn, NEG_INF)
        update(sn, jnp.zeros((1, LANES), F32), nck, ndv)
        l_col = _to_col(l_s[...])
        om_ref[0] = accm_s[0:_SM_ROWS, :] / l_col[0:_SM_ROWS]
        od_all = accd_s[...] / l_col
        hd = DIFF_HEADS * 8
        lam = _lam(lq1_ref[...], lk1_ref[...], lq2_ref[...], lk2_ref[...], lam_init)
        od = od_all[_SM_ROWS:_SM_ROWS + hd] - lam * od_all[_SM_ROWS + hd:_SM_ROWS + 2 * hd]
        od_ref[0] = _rms(od, dn_ref[...]) * (1.0 - lam_init)


def _sample_attn(page_table, wq, nck, nkr, ndk, ndv, lq1, lk1, lq2, lk2, dn,
                 cache_ckv, cache_krope, cache_dk, cache_dv, li, lam_init, ppc):
    ns, npages = page_table.shape
    page = cache_ckv.shape[2]
    nc = npages // ppc
    ck = ppc * page
    hd = DIFF_HEADS * 8
    per_seq = lambda shape: pl.BlockSpec((1,) + shape, lambda s, c, pt: (s, 0, 0))
    vec = lambda w: pl.BlockSpec((1, w), lambda s, c, pt: (0, 0))
    hbm = pl.BlockSpec(memory_space=pl.ANY)
    grid_spec = pltpu.PrefetchScalarGridSpec(
        num_scalar_prefetch=1,
        grid=(ns, nc),
        in_specs=[per_seq((_WQ_END, LANES)), per_seq((NEW_PAD, KV_LORA)), per_seq((NEW_PAD, ROPE_PAD)),
                  per_seq((NEW_PAD, 2 * DIFF_HALF)), per_seq((NEW_PAD, DIFF_V)),
                  vec(DIFF_HALF), vec(DIFF_HALF), vec(DIFF_HALF), vec(DIFF_HALF), vec(DIFF_V),
                  hbm, hbm, hbm, hbm],
        out_specs=[per_seq((_SM_ROWS, KV_LORA)), per_seq((hd, DIFF_V))],
        scratch_shapes=[pltpu.VMEM((2, ppc, page, KV_LORA), F32),
                        pltpu.VMEM((2, MLA_ROPE, ck), F32),
                        pltpu.VMEM((2, ppc, page, 2 * DIFF_HALF), F32),
                        pltpu.VMEM((2, ppc, page, DIFF_V), F32),
                        pltpu.SemaphoreType.DMA((2, 4)),
                        pltpu.VMEM((1, LANES), F32), pltpu.VMEM((1, LANES), F32),
                        pltpu.VMEM((LANES, KV_LORA), F32), pltpu.VMEM((LANES, DIFF_V), F32),
                        pltpu.VMEM((ck, LANES), F32)],
    )
    return pl.pallas_call(
        functools.partial(_samp_body, li=li, ppc=ppc, nc=nc, page=page, past=npages * page,
                          lam_init=lam_init),
        grid_spec=grid_spec,
        out_shape=[jax.ShapeDtypeStruct((ns, _SM_ROWS, KV_LORA), F32),
                   jax.ShapeDtypeStruct((ns, hd, DIFF_V), F32)],
        compiler_params=_params("arbitrary", "arbitrary"),
    )(page_table, wq, nck, nkr, ndk, ndv, lq1, lk1, lq2, lk2, dn,
      cache_ckv, cache_krope, cache_dk, cache_dv)


def _mixout_body(h_ref, ol_ref, conv_ref, diff_ref, wuv_ref, wo_ref, post_ref, o_ref):
    parts = [_dot(ol_ref[hh], wuv_ref[hh]).astype(BF16) for hh in range(MLA_HEADS)]
    mla = jnp.concatenate(parts, axis=1)
    n_mla = mla.shape[1]
    n_conv = n_mla + CONV_WIDTH
    mixed = (_dot(mla, wo_ref[0:n_mla, :]) + _dot(conv_ref[...], wo_ref[n_mla:n_conv, :])
             + _dot(diff_ref[...], wo_ref[n_conv:, :]))
    o_ref[...] = h_ref[...] + _rms(mixed, post_ref[...])


def _mixout(h, o_lat, conv_o, diff_o, wuv, wo, post_g, tm):
    m, d = h.shape
    tok = lambda w: pl.BlockSpec((tm, w), lambda i: (i, 0))
    return pl.pallas_call(
        _mixout_body,
        grid=(m // tm,),
        in_specs=[tok(d), pl.BlockSpec((MLA_HEADS, tm, KV_LORA), lambda i: (0, i, 0)),
                  tok(CONV_WIDTH), tok(DIFF_HEADS * DIFF_V),
                  _const_spec(wuv.shape), _const_spec(wo.shape), _const_spec((1, d))],
        out_specs=tok(d),
        out_shape=jax.ShapeDtypeStruct((m, d), F32),
        compiler_params=_params("arbitrary"),
    )(h, o_lat, conv_o, diff_o, wuv, wo, post_g)


def _swap_halves(w):
    half = w.shape[-1] // 2
    return jnp.concatenate([w[..., half:], w[..., :half]], axis=-1)


def _pad_last(w, width):
    return jnp.pad(w, [(0, 0)] * (w.ndim - 1) + [(0, width - w.shape[-1])])


def _pack_w_in(w):
    s = np.cumsum([0, Q_LORA, KV_LORA, MLA_ROPE, CONV_WIDTH, CONV_WIDTH, CONV_WIDTH,
                   DIFF_HEADS * 2 * DIFF_HALF, 2 * DIFF_HALF, DIFF_V])
    cq, ckv, kr, cb, cc, cx, dq, dk, dv = [w[:, s[k]:s[k + 1]] for k in range(9)]
    return jnp.concatenate([cq, ckv, _pad_last(kr, ROPE_PAD), _pad_last(_swap_halves(kr), ROPE_PAD),
                            cb, cc, cx, dq, dk, dv], axis=1).astype(BF16)


def _rope_tables(pos):
    half = MLA_ROPE // 2
    freqs = ROPE_THETA ** (-jnp.arange(half, dtype=F32) / half)
    ang = pos.astype(F32)[:, None] * freqs[None, :]
    cos, sin = jnp.cos(ang), jnp.sin(ang)
    return (jnp.concatenate([cos, cos, cos, cos], axis=1),
            jnp.concatenate([-sin, sin, -sin, sin], axis=1))


def _tile(n, pref):
    for t in pref:
        if n % t == 0:
            return t
    return n


def kernel(x_prompt, x_sample, cache_ckv, cache_krope, cache_diff_k, cache_diff_v, state_conv, page_table, ffn1_pre_g, ffn1_post_g, ffn1_w_gate, ffn1_w_up, ffn1_w_down, mix_pre_g, mix_post_g, w_in, q_norm_g, kv_norm_g, w_uq, w_uk, w_uv, conv_w, diff_lq1, diff_lk1, diff_lq2, diff_lk2, diff_norm_g, w_o, ffn2_pre_g, ffn2_post_g, ffn2_w_gate, ffn2_w_up, ffn2_w_down):
    nb, seq, d = x_prompt.shape
    ns, dseq, _ = x_sample.shape
    depth = w_in.shape[0]
    npages, page = page_table.shape[1], cache_ckv.shape[2]
    past = npages * page
    mp, msamp = nb * seq, ns * dseq
    m = mp + msamp
    assert dseq == 8 and MLA_HEADS * dseq == _SM_ROWS

    tm_ffn = _tile(m, (512, 256, 128))
    tf = _tile(ffn1_w_gate.shape[2], (512, 256, 128))
    tm_mix = _tile(m, (256, 128))
    tq = _tile(seq, (128,))
    tk = _tile(seq, (512, 256, 128))
    ppc = _tile(npages, (32, 16, 8, 4, 2, 1))

    pos = jnp.concatenate([jnp.tile(jnp.arange(seq, dtype=jnp.int32), nb),
                           jnp.tile(past + jnp.arange(dseq, dtype=jnp.int32), ns)])
    tab_c, tab_s = _rope_tables(pos)

    h = jnp.concatenate([x_prompt.reshape(mp, d), x_sample.reshape(msamp, d)], axis=0)
    cache_krope_t = jnp.swapaxes(cache_krope, 2, 3)
    row = lambda v: v.reshape(1, -1)
    states = []
    for li in range(depth):
        lam_init = 0.8 - 0.6 * math.exp(-0.3 * li)
        uq = w_uq[li]
        wqn = uq[:, :, :MLA_NOPE].reshape(Q_LORA, MLA_HEADS * MLA_NOPE).astype(BF16)
        uqr = uq[:, :, MLA_NOPE:]
        wqr = _pad_last(uqr, ROPE_PAD).reshape(Q_LORA, MLA_HEADS * ROPE_PAD).astype(BF16)
        wqs = _pad_last(_swap_halves(uqr), ROPE_PAD).reshape(Q_LORA, MLA_HEADS * ROPE_PAD).astype(BF16)
        wuk = jnp.transpose(w_uk[li], (1, 2, 0)).astype(BF16)
        wuv = jnp.transpose(w_uv[li], (1, 0, 2)).astype(BF16)
        win = _pack_w_in(w_in[li])
        lq1, lk1, lq2, lk2 = row(diff_lq1[li]), row(diff_lk1[li]), row(diff_lq2[li]), row(diff_lk2[li])
        dn = row(diff_norm_g[li])

        h = _ffn(h, row(ffn1_pre_g[li]), ffn1_w_gate[li].astype(BF16), ffn1_w_up[li].astype(BF16),
                 ffn1_w_down[li].astype(BF16), row(ffn1_post_g[li]), tm_ffn, tf)

        (ql, qr, ckv, kr, ckvb, krb, cb, u, dq, dk, dv, dkb, dvb) = _mixin(
            h, row(mix_pre_g[li]), win, row(q_norm_g[li]), row(kv_norm_g[li]),
            wqn, wqr, wqs, wuk, tab_c, tab_s, tm_mix)

        cw = conv_w[li]
        u_p = u[:mp].reshape(nb, seq, CONV_WIDTH)
        xpad_p = jnp.concatenate([jnp.zeros((nb, 8, CONV_WIDTH), F32), u_p], axis=1)
        conv_p = _conv(xpad_p, cb[:mp].reshape(nb, seq, CONV_WIDTH), cw)
        u_s = u[mp:].reshape(ns, dseq, CONV_WIDTH)
        xs = jnp.concatenate([jnp.zeros((ns, 8 - (CONV_K - 1), CONV_WIDTH), F32), state_conv[li], u_s], axis=1)
        xpad_s = jnp.concatenate([jnp.zeros((1, 8, CONV_WIDTH), F32), xs.reshape(1, ns * 16, CONV_WIDTH)], axis=1)
        cb_s = jnp.concatenate([jnp.zeros((ns, 8, CONV_WIDTH), F32), cb[mp:].reshape(ns, dseq, CONV_WIDTH)], axis=1)
        conv_s = _conv(xpad_s, cb_s.reshape(1, ns * 16, CONV_WIDTH), cw)
        conv_s = conv_s.reshape(ns, 16, CONV_WIDTH)[:, 8:, :]
        conv_o = jnp.concatenate([conv_p.reshape(mp, CONV_WIDTH), conv_s.reshape(msamp, CONV_WIDTH)], axis=0)
        conv_state_p = u_p[:, seq - (CONV_K - 1):, :]
        conv_state_s = xs[:, 16 - (CONV_K - 1):, :]

        ol_p = _prompt_mla(ql, qr, ckvb, krb, nb, seq, tq, tk)
        diff_p = _prompt_diff(dq, dkb, dvb, lq1, lk1, lq2, lk2, dn, lam_init, nb, seq, tq, tk)

        def per_seq_t(a, nh):
            w = a.shape[-1]
            a = a[:, mp:, :].reshape(nh, ns, dseq, w)
            return jnp.transpose(a, (1, 3, 0, 2)).reshape(ns, w, nh * dseq)

        dq_t = per_seq_t(dq, DIFF_HEADS)
        lane_lo = (jnp.arange(2 * DIFF_HALF) < DIFF_HALF)[None, :, None]
        zq = jnp.zeros_like(dq_t)
        wq_m = jnp.concatenate([per_seq_t(ql, MLA_HEADS), per_seq_t(qr, MLA_HEADS)], axis=1)
        wq_m = jnp.concatenate([wq_m, jnp.zeros((ns, _WQ_DK, LANES - _SM_ROWS), BF16)], axis=2)
        wq_d = jnp.concatenate([jnp.zeros((ns, 2 * DIFF_HALF, _SM_ROWS), BF16),
                                jnp.where(lane_lo, dq_t, zq), jnp.where(lane_lo, zq, dq_t)], axis=2)
        wq = jnp.concatenate([wq_m, wq_d], axis=1)

        def new_rows(a):
            a = a[mp:].reshape(ns, dseq, a.shape[-1])
            return jnp.pad(a, ((0, 0), (0, NEW_PAD - dseq), (0, 0)))

        om, od = _sample_attn(page_table, wq, new_rows(ckvb), new_rows(krb), new_rows(dkb), new_rows(dvb),
                              lq1, lk1, lq2, lk2, dn, cache_ckv, cache_krope_t, cache_diff_k, cache_diff_v,
                              li, lam_init, ppc)
        ol_s = jnp.transpose(om.reshape(ns, MLA_HEADS, dseq, KV_LORA), (1, 0, 2, 3))
        ol = jnp.concatenate([ol_p, ol_s.reshape(MLA_HEADS, msamp, KV_LORA).astype(BF16)], axis=1)
        diff_s = jnp.transpose(od.reshape(ns, DIFF_HEADS, dseq, DIFF_V), (0, 2, 1, 3))
        diff_o = jnp.concatenate([diff_p, diff_s.reshape(msamp, DIFF_HEADS * DIFF_V).astype(BF16)], axis=0)

        h = _mixout(h, ol, conv_o, diff_o, wuv, w_o[li].astype(BF16), row(mix_post_g[li]), tm_mix)
        h = _ffn(h, row(ffn2_pre_g[li]), ffn2_w_gate[li].astype(BF16), ffn2_w_up[li].astype(BF16),
                 ffn2_w_down[li].astype(BF16), row(ffn2_post_g[li]), tm_ffn, tf)
        states.append((ckv, kr, dk, dv, conv_state_p, conv_state_s))

    def stack(k, lo, hi, shape):
        return jnp.stack([s[k][lo:hi].reshape(shape + (s[k].shape[-1],)) for s in states])

    p_shape, s_shape = (nb, seq), (ns, dseq)
    return (h[:mp].reshape(nb, seq, d), h[mp:].reshape(ns, dseq, d),
            stack(0, 0, mp, p_shape), stack(1, 0, mp, p_shape), stack(2, 0, mp, p_shape), stack(3, 0, mp, p_shape),
            jnp.stack([s[4] for s in states]),
            stack(0, mp, m, s_shape), stack(1, mp, m, s_shape), stack(2, mp, m, s_shape), stack(3, mp, m, s_shape),
            jnp.stack([s[5] for s in states]))
```

```python
import functools
import math

import numpy as np
import jax
import jax.numpy as jnp
from jax import lax
from jax.experimental import pallas as pl
from jax.experimental.pallas import tpu as pltpu

F32 = jnp.float32
BF16 = jnp.bfloat16

MLA_HEADS = 8
MLA_NOPE = 128
MLA_ROPE = 64
KV_LORA = 256
Q_LORA = 512
CONV_WIDTH = 512
CONV_K = 3
DIFF_HEADS = 4
DIFF_HALF = 64
DIFF_V = 2 * DIFF_HALF
ROPE_THETA = 10000.0
NORM_EPS = 1e-6
NEG_INF = -1e30
LANES = 128
ROPE_PAD = LANES
VMEM_LIMIT = 56 * 1024 * 1024

MLA_SCALE = (MLA_NOPE + MLA_ROPE) ** -0.5
DIFF_SCALE = DIFF_HALF ** -0.5

_C_CQ = 0
_C_CKV = _C_CQ + Q_LORA
_C_KR = _C_CKV + KV_LORA
_C_KRS = _C_KR + ROPE_PAD
_C_CB = _C_KRS + ROPE_PAD
_C_CC = _C_CB + CONV_WIDTH
_C_CX = _C_CC + CONV_WIDTH
_C_DQ = _C_CX + CONV_WIDTH
_C_DK = _C_DQ + DIFF_HEADS * 2 * DIFF_HALF
_C_DV = _C_DK + 2 * DIFF_HALF
_C_END = _C_DV + DIFF_V


def _params(*sem):
    return pltpu.CompilerParams(dimension_semantics=sem, vmem_limit_bytes=VMEM_LIMIT)


def _rms(x, g):
    ms = jnp.mean(x * x, axis=-1, keepdims=True)
    return x * lax.rsqrt(ms + NORM_EPS) * g


def _dot(a, b):
    return jnp.dot(a, b, preferred_element_type=F32)


def _dot_nt(a, b):
    return lax.dot_general(a, b, (((1,), (1,)), ((), ())), preferred_element_type=F32)


def _dot_tn(a, b):
    return lax.dot_general(a, b, (((0,), (0,)), ((), ())), preferred_element_type=F32)


def _lam(lq1, lk1, lq2, lk2, lam_init):
    a = jnp.sum(lq1 * lk1, axis=-1, keepdims=True)
    b = jnp.sum(lq2 * lk2, axis=-1, keepdims=True)
    return jnp.exp(a) - jnp.exp(b) + lam_init


def _const_spec(shape):
    nd = len(shape)
    return pl.BlockSpec(shape, lambda *_: (0,) * nd, pipeline_mode=pl.Buffered(1))


def _ffn_body(x_ref, pre_ref, wg_ref, wu_ref, wd_ref, post_ref, o_ref, xn_ref, *, nf):
    f = pl.program_id(1)

    @pl.when(f == 0)
    def _():
        xn_ref[...] = _rms(x_ref[...], pre_ref[...]).astype(BF16)
        o_ref[...] = jnp.zeros_like(o_ref)

    xn = xn_ref[...]
    g = _dot(xn, wg_ref[...])
    u = _dot(xn, wu_ref[...])
    a = (g * (1.0 / (1.0 + jnp.exp(-g)))) * u
    o_ref[...] += _dot(a.astype(BF16), wd_ref[...])

    @pl.when(f == nf - 1)
    def _():
        o_ref[...] = x_ref[...] + 0.5 * _rms(o_ref[...], post_ref[...])


def _ffn(x, pre_g, wg, wu, wd, post_g, li, tm, tf):
    m, d = x.shape
    nf = wg.shape[2] // tf
    return pl.pallas_call(
        functools.partial(_ffn_body, nf=nf),
        grid=(m // tm, nf),
        in_specs=[
            pl.BlockSpec((tm, d), lambda i, f: (i, 0)),
            pl.BlockSpec((1, d), lambda i, f: (0, 0)),
            pl.BlockSpec((None, d, tf), lambda i, f: (li, 0, f)),
            pl.BlockSpec((None, d, tf), lambda i, f: (li, 0, f)),
            pl.BlockSpec((None, tf, d), lambda i, f: (li, f, 0)),
            pl.BlockSpec((1, d), lambda i, f: (0, 0)),
        ],
        out_specs=pl.BlockSpec((tm, d), lambda i, f: (i, 0)),
        out_shape=jax.ShapeDtypeStruct((m, d), F32),
        scratch_shapes=[pltpu.VMEM((tm, d), BF16)],
        compiler_params=_params("arbitrary", "arbitrary"),
    )(x, pre_g, wg, wu, wd, post_g)


def _mixin_body(h_ref, pre_ref, win_ref, qn_ref, kvn_ref, wqn_ref, wqr_ref, wqs_ref, wuk_ref,
                tc_ref, ts_ref,
                ql_ref, qr_ref, ckv_ref, kr_ref, ckvb_ref, krb_ref, cb_ref, u_ref,
                dq_ref, dk_ref, dv_ref, dkb_ref, dvb_ref):
    n = _rms(h_ref[...], pre_ref[...]).astype(BF16)
    tab_c = tc_ref[...]
    tab_s = ts_ref[...]

    def proj(c0, c1):
        return _dot(n, win_ref[:, c0:c1])

    cq = _rms(proj(_C_CQ, _C_CKV), qn_ref[...]).astype(BF16)
    qn = _dot(cq, wqn_ref[...]).astype(BF16)
    qr = _dot(cq, wqr_ref[...])
    qs = _dot(cq, wqs_ref[...])
    for hh in range(MLA_HEADS):
        q_lat = _dot(qn[:, hh * MLA_NOPE:(hh + 1) * MLA_NOPE], wuk_ref[hh])
        ql_ref[hh] = (q_lat * MLA_SCALE).astype(BF16)
        sl = slice(hh * ROPE_PAD, (hh + 1) * ROPE_PAD)
        qr_ref[hh] = ((qr[:, sl] * tab_c + qs[:, sl] * tab_s) * MLA_SCALE).astype(BF16)
    ckv = _rms(proj(_C_CKV, _C_KR), kvn_ref[...])
    ckv_ref[...] = ckv
    ckvb_ref[...] = ckv.astype(BF16)
    kr = proj(_C_KR, _C_KRS) * tab_c + proj(_C_KRS, _C_CB) * tab_s
    kr_ref[...] = kr[:, :MLA_ROPE]
    krb_ref[...] = kr.astype(BF16)
    cb_ref[...] = proj(_C_CB, _C_CC)
    u_ref[...] = proj(_C_CC, _C_CX) * proj(_C_CX, _C_DQ)
    dq = proj(_C_DQ, _C_DK) * DIFF_SCALE
    for hh in range(DIFF_HEADS):
        dq_ref[hh] = dq[:, hh * 2 * DIFF_HALF:(hh + 1) * 2 * DIFF_HALF].astype(BF16)
    dk = proj(_C_DK, _C_DV)
    dv = proj(_C_DV, _C_END)
    dk_ref[...] = dk
    dv_ref[...] = dv
    dkb_ref[...] = dk.astype(BF16)
    dvb_ref[...] = dv.astype(BF16)


def _mixin(h, pre_g, win, qn_g, kvn_g, wqn, wqr, wqs, wuk, tab_c, tab_s, tm):
    m, d = h.shape
    tok = lambda w: pl.BlockSpec((tm, w), lambda i: (i, 0))
    headmajor = lambda nh, w: pl.BlockSpec((nh, tm, w), lambda i: (0, i, 0))
    sds = jax.ShapeDtypeStruct
    return pl.pallas_call(
        _mixin_body,
        grid=(m // tm,),
        in_specs=[tok(d), _const_spec((1, d)), _const_spec(win.shape), _const_spec((1, Q_LORA)),
                  _const_spec((1, KV_LORA)), _const_spec(wqn.shape), _const_spec(wqr.shape),
                  _const_spec(wqs.shape), _const_spec(wuk.shape), tok(LANES), tok(LANES)],
        out_specs=[headmajor(MLA_HEADS, KV_LORA), headmajor(MLA_HEADS, ROPE_PAD),
                   tok(KV_LORA), tok(MLA_ROPE), tok(KV_LORA), tok(ROPE_PAD),
                   tok(CONV_WIDTH), tok(CONV_WIDTH),
                   headmajor(DIFF_HEADS, 2 * DIFF_HALF), tok(2 * DIFF_HALF), tok(DIFF_V),
                   tok(2 * DIFF_HALF), tok(DIFF_V)],
        out_shape=[sds((MLA_HEADS, m, KV_LORA), BF16), sds((MLA_HEADS, m, ROPE_PAD), BF16),
                   sds((m, KV_LORA), F32), sds((m, MLA_ROPE), F32),
                   sds((m, KV_LORA), BF16), sds((m, ROPE_PAD), BF16),
                   sds((m, CONV_WIDTH), F32), sds((m, CONV_WIDTH), F32),
                   sds((DIFF_HEADS, m, 2 * DIFF_HALF), BF16),
                   sds((m, 2 * DIFF_HALF), F32), sds((m, DIFF_V), F32),
                   sds((m, 2 * DIFF_HALF), BF16), sds((m, DIFF_V), BF16)],
        compiler_params=_params("arbitrary"),
    )(h, pre_g, win, qn_g, kvn_g, wqn, wqr, wqs, wuk, tab_c, tab_s)


def _conv_body(xp_ref, cb_ref, cw_ref, o_ref):
    t = cb_ref.shape[1]
    conv = (cw_ref[0:1, :] * xp_ref[0, 6:6 + t, :]
            + cw_ref[1:2, :] * xp_ref[0, 7:7 + t, :]
            + cw_ref[2:3, :] * xp_ref[0, 8:8 + t, :])
    o_ref[0] = (cb_ref[0] * conv).astype(BF16)


def _conv(xpad, cb, cw):
    b, t, c = cb.shape
    return pl.pallas_call(
        _conv_body,
        grid=(b, c // LANES),
        in_specs=[pl.BlockSpec((1, t + 8, LANES), lambda i, j: (i, 0, j)),
                  pl.BlockSpec((1, t, LANES), lambda i, j: (i, 0, j)),
                  pl.BlockSpec((CONV_K, LANES), lambda i, j: (0, j))],
        out_specs=pl.BlockSpec((1, t, LANES), lambda i, j: (i, 0, j)),
        out_shape=jax.ShapeDtypeStruct((b, t, c), BF16),
        compiler_params=_params("arbitrary", "arbitrary"),
    )(xpad, cb, cw)


def _causal_softmax_pv(i, tq, tk, score, v_ref, s_s, mb_s, l_s, acc_s):
    rows = acc_s.shape[0]
    nblk = tk // LANES
    nfull = (i * tq + 1) // tk
    nneed = ((i + 1) * tq + tk - 1) // tk
    mb_s[...] = jnp.full_like(mb_s, NEG_INF)

    def pass1(masked):
        def body(j, c):
            k0 = pl.multiple_of(j * tk, tk)
            s = score(j, k0, masked)
            if masked:
                t = lax.broadcasted_iota(jnp.int32, (rows, tk), 0) % tq
                col = lax.broadcasted_iota(jnp.int32, (rows, tk), 1)
                s = jnp.where(col + k0 <= t + i * tq, s, NEG_INF)
            s_s[j] = s
            mr = mb_s[...]
            for b in range(nblk):
                mr = jnp.maximum(mr, s[:, b * LANES:(b + 1) * LANES])
            mb_s[...] = mr
            return c
        return body

    lax.fori_loop(0, nfull, pass1(False), 0)
    lax.fori_loop(nfull, nneed, pass1(True), 0)
    m = jnp.max(mb_s[...], axis=1, keepdims=True)
    mb_s[...] = jnp.broadcast_to(m, mb_s.shape)
    l_s[...] = jnp.zeros_like(l_s)
    acc_s[...] = jnp.zeros_like(acc_s)

    def pass2(j, c):
        k0 = pl.multiple_of(j * tk, tk)
        mb = mb_s[...]
        lr = l_s[...]
        parts = []
        for b in range(nblk):
            pb = jnp.exp(s_s[j, :, b * LANES:(b + 1) * LANES] - mb)
            lr = lr + pb
            parts.append(pb.astype(BF16))
        l_s[...] = lr
        acc_s[...] += _dot(jnp.concatenate(parts, axis=1), v_ref[pl.ds(k0, tk), :])
        return c

    lax.fori_loop(0, nneed, pass2, 0)
    return acc_s[...] / jnp.sum(l_s[...], axis=1, keepdims=True)


def _pmla_body(ql_ref, qr_ref, ckv_ref, kr_ref, o_ref, s_s, mb_s, l_s, acc_s, *, tq, tk):
    i = pl.program_id(1)
    rows = MLA_HEADS * tq
    q = ql_ref[...].reshape(rows, KV_LORA)
    qr = qr_ref[...].reshape(rows, ROPE_PAD)

    def score(j, k0, masked):
        return _dot_nt(q, ckv_ref[pl.ds(k0, tk), :]) + _dot_nt(qr, kr_ref[pl.ds(k0, tk), :])

    o = _causal_softmax_pv(i, tq, tk, score, ckv_ref, s_s, mb_s, l_s, acc_s)
    o_ref[...] = o.astype(BF16).reshape(MLA_HEADS, tq, KV_LORA)


def _prompt_mla(ql, qr, ckvb, krb, nb, seq, tq, tk):
    nq = seq // tq
    rows = MLA_HEADS * tq
    return pl.pallas_call(
        functools.partial(_pmla_body, tq=tq, tk=tk),
        grid=(nb, nq),
        in_specs=[pl.BlockSpec((MLA_HEADS, tq, KV_LORA), lambda b, i: (0, b * nq + i, 0)),
                  pl.BlockSpec((MLA_HEADS, tq, ROPE_PAD), lambda b, i: (0, b * nq + i, 0)),
                  pl.BlockSpec((seq, KV_LORA), lambda b, i: (b, 0)),
                  pl.BlockSpec((seq, ROPE_PAD), lambda b, i: (b, 0))],
        out_specs=pl.BlockSpec((MLA_HEADS, tq, KV_LORA), lambda b, i: (0, b * nq + i, 0)),
        out_shape=jax.ShapeDtypeStruct((MLA_HEADS, nb * seq, KV_LORA), BF16),
        scratch_shapes=[pltpu.VMEM((seq // tk, rows, tk), F32), pltpu.VMEM((rows, LANES), F32),
                        pltpu.VMEM((rows, LANES), F32), pltpu.VMEM((rows, KV_LORA), F32)],
        compiler_params=_params("arbitrary", "arbitrary"),
    )(ql, qr, ckvb, krb)


def _head_slope(h):
    return jnp.where(h == 0, 2.0 ** -2, jnp.where(h == 1, 2.0 ** -4,
                                                   jnp.where(h == 2, 2.0 ** -6, 2.0 ** -8))).astype(F32)


def _pdiff_body(dq_ref, k_ref, v_ref, lq1_ref, lk1_ref, lq2_ref, lk2_ref, dn_ref, o_ref,
                s_s, mb_s, l_s, acc_s, b0_s, *, tq, tk, lam_init):
    i = pl.program_id(1)
    half = DIFF_HEADS * tq
    rows = 2 * half

    @pl.when((pl.program_id(0) == 0) & (i == 0))
    def _():
        r = lax.broadcasted_iota(jnp.int32, (rows, tk), 0)
        col = lax.broadcasted_iota(jnp.int32, (rows, tk), 1)
        b0_s[...] = _head_slope((r // tq) % DIFF_HEADS) * (col - r % tq).astype(F32)

    x = dq_ref[...].reshape(half, 2 * DIFF_HALF)
    lane = lax.broadcasted_iota(jnp.int32, x.shape, 1)
    zero = jnp.zeros_like(x)
    q = jnp.concatenate([jnp.where(lane < DIFF_HALF, x, zero),
                         jnp.where(lane >= DIFF_HALF, x, zero)], axis=0)
    slope = _head_slope((lax.broadcasted_iota(jnp.int32, (rows, LANES), 0) // tq) % DIFF_HEADS)

    def score(j, k0, masked):
        off = slope * (k0 - i * tq).astype(F32)
        return (_dot_nt(q, k_ref[pl.ds(k0, tk), :]) + b0_s[...]
                + jnp.concatenate([off] * (tk // LANES), axis=1))

    o = _causal_softmax_pv(i, tq, tk, score, v_ref, s_s, mb_s, l_s, acc_s)
    lam = _lam(lq1_ref[...], lk1_ref[...], lq2_ref[...], lk2_ref[...], lam_init)
    od = _rms(o[:half] - lam * o[half:], dn_ref[...]) * (1.0 - lam_init)
    for hh in range(DIFF_HEADS):
        o_ref[:, hh * DIFF_V:(hh + 1) * DIFF_V] = od[hh * tq:(hh + 1) * tq].astype(BF16)


def _prompt_diff(dq, dkb, dvb, lq1, lk1, lq2, lk2, dn, lam_init, nb, seq, tq, tk):
    nq = seq // tq
    rows = 2 * DIFF_HEADS * tq
    vec = lambda w: pl.BlockSpec((1, w), lambda b, i: (0, 0))
    return pl.pallas_call(
        functools.partial(_pdiff_body, tq=tq, tk=tk, lam_init=lam_init),
        grid=(nb, nq),
        in_specs=[pl.BlockSpec((DIFF_HEADS, tq, 2 * DIFF_HALF), lambda b, i: (0, b * nq + i, 0)),
                  pl.BlockSpec((seq, 2 * DIFF_HALF), lambda b, i: (b, 0)),
                  pl.BlockSpec((seq, DIFF_V), lambda b, i: (b, 0)),
                  vec(DIFF_HALF), vec(DIFF_HALF), vec(DIFF_HALF), vec(DIFF_HALF), vec(DIFF_V)],
        out_specs=pl.BlockSpec((tq, DIFF_HEADS * DIFF_V), lambda b, i: (b * nq + i, 0)),
        out_shape=jax.ShapeDtypeStruct((nb * seq, DIFF_HEADS * DIFF_V), BF16),
        scratch_shapes=[pltpu.VMEM((seq // tk, rows, tk), F32), pltpu.VMEM((rows, LANES), F32),
                        pltpu.VMEM((rows, LANES), F32), pltpu.VMEM((rows, DIFF_V), F32),
                        pltpu.VMEM((rows, tk), F32)],
        compiler_params=_params("arbitrary", "arbitrary"),
    )(dq, dkb, dvb, lq1, lk1, lq2, lk2, dn)


_SM_ROWS = MLA_HEADS * 8
_WQ_KR = KV_LORA
_WQ_DK = KV_LORA + ROPE_PAD
_WQ_END = _WQ_DK + 2 * DIFF_HALF
NEW_PAD = 16


def _lane_slope(lane):
    return jnp.where(lane >= _SM_ROWS, _head_slope(((lane - _SM_ROWS) // 8) % DIFF_HEADS), 0.0)


def _to_col(row):
    r = lax.broadcasted_iota(jnp.int32, (LANES, LANES), 0)
    c = lax.broadcasted_iota(jnp.int32, (LANES, LANES), 1)
    return jnp.sum(jnp.where(r == c, jnp.broadcast_to(row, (LANES, LANES)), 0.0), axis=1, keepdims=True)


def _samp_body(pt_ref, wq_ref, nck_ref, nkr_ref, ndk_ref, ndv_ref,
               lq1_ref, lk1_ref, lq2_ref, lk2_ref, dn_ref,
               cckv, ckr, cdk, cdv, om_ref, od_ref,
               bckv, bkr, bdk, bdv, sems, m_s, l_s, accm_s, accd_s, b0_s,
               *, li, ppc, nc, page, past, lam_init):
    s_id = pl.program_id(0)
    c_id = pl.program_id(1)
    ns = pl.num_programs(0)
    g = s_id * nc + c_id
    slot = g % 2
    ck = ppc * page
    caches = (cckv, ckr, cdk, cdv)
    bufs = (bckv, bkr, bdk, bdv)

    def copies(s, c, sl, from_table):
        out = []
        for p in range(ppc):
            pg = pt_ref[s, c * ppc + p] if from_table else 0
            for a in range(4):
                dst = bkr.at[sl, :, pl.ds(p * page, page)] if a == 1 else bufs[a].at[sl, p]
                out.append(pltpu.make_async_copy(caches[a].at[li, pg], dst, sems.at[sl, a]))
        return out

    def start_all(cps):
        for n, cp in enumerate(cps):
            cp.start(priority=(n // 4) % 2)

    @pl.when(g == 0)
    def _():
        start_all(copies(s_id, c_id, slot, True))
        key = lax.broadcasted_iota(jnp.int32, (ck, LANES), 0)
        lane = lax.broadcasted_iota(jnp.int32, (ck, LANES), 1)
        b0_s[...] = _lane_slope(lane) * key.astype(F32)

    @pl.when(g + 1 < ns * nc)
    def _():
        last = c_id == nc - 1
        s_n = jnp.where(last, s_id + 1, s_id)
        c_n = jnp.where(last, 0, c_id + 1)
        start_all(copies(s_n, c_n, 1 - slot, True))

    for cp in copies(0, 0, slot, False):
        cp.wait()

    @pl.when(c_id == 0)
    def _():
        m_s[...] = jnp.full_like(m_s, NEG_INF)
        l_s[...] = jnp.zeros_like(l_s)
        accm_s[...] = jnp.zeros_like(accm_s)
        accd_s[...] = jnp.zeros_like(accd_s)

    lane1 = lax.broadcasted_iota(jnp.int32, (1, LANES), 1)
    slope1 = _lane_slope(lane1)
    tok1 = lane1 % 8

    def update(state, st, off, kv, vd):
        m_prev, l_prev, acc_m, acc_d = state
        m_new = jnp.maximum(m_prev, jnp.max(st, axis=0, keepdims=True) + off)
        alpha = jnp.exp(m_prev - m_new)
        p = jnp.exp(st - (m_new - off))
        l_new = alpha * l_prev + jnp.sum(p, axis=0, keepdims=True)
        pb = p.astype(BF16)
        a_col = _to_col(alpha)
        return (m_new, l_new, a_col * acc_m + _dot_tn(pb, kv), a_col * acc_d + _dot_tn(pb, vd))

    def load_state():
        return (m_s[...], l_s[...], accm_s[...], accd_s[...])

    kc = bckv[slot].reshape(ck, KV_LORA).astype(BF16)
    krt = bkr[slot].astype(BF16)
    kdc = bdk[slot].reshape(ck, 2 * DIFF_HALF).astype(BF16)
    vdc = bdv[slot].reshape(ck, DIFF_V).astype(BF16)
    st = (_dot(kc, wq_ref[0, 0:_WQ_KR, :]) + _dot_tn(krt, wq_ref[0, _WQ_KR:_WQ_KR + MLA_ROPE, :])
          + _dot(kdc, wq_ref[0, _WQ_DK:_WQ_END, :]) + b0_s[...])
    off = slope1 * (c_id * ck - past - tok1).astype(F32)
    m_s[...], l_s[...], accm_s[...], accd_s[...] = update(load_state(), st, off, kc, vdc)

    @pl.when(c_id == nc - 1)
    def _():
        nck = nck_ref[0]
        ndv = ndv_ref[0]
        sn = (_dot(nck, wq_ref[0, 0:_WQ_KR, :]) + _dot(nkr_ref[0], wq_ref[0, _WQ_KR:_WQ_DK, :])
              + _dot(ndk_ref[0], wq_ref[0, _WQ_DK:_WQ_END, :]))
        j = lax.broadcasted_iota(jnp.int32, (NEW_PAD, LANES), 0)
        lane = lax.broadcasted_iota(jnp.int32, (NEW_PAD, LANES), 1)
        tok = lane % 8
        sn = sn + _lane_slope(lane) * (j - tok).astype(F32)
        sn = jnp.where(j <= tok, sn, NEG_INF)
        _, l_fin, acc_m, acc_d = update(load_state(), sn, jnp.zeros((1, LANES), F32), nck, ndv)
        l_col = _to_col(l_fin)
        om_ref[0] = acc_m[0:_SM_ROWS, :] / l_col[0:_SM_ROWS]
        od_all = acc_d / l_col
        hd = DIFF_HEADS * 8
        lam = _lam(lq1_ref[...], lk1_ref[...], lq2_ref[...], lk2_ref[...], lam_init)
        od = od_all[_SM_ROWS:_SM_ROWS + hd] - lam * od_all[_SM_ROWS + hd:_SM_ROWS + 2 * hd]
        od_ref[0] = _rms(od, dn_ref[...]) * (1.0 - lam_init)


def _sample_attn(page_table, wq, nck, nkr, ndk, ndv, lq1, lk1, lq2, lk2, dn,
                 cache_ckv, cache_krope, cache_dk, cache_dv, li, lam_init, ppc):
    ns, npages = page_table.shape
    page = cache_ckv.shape[2]
    nc = npages // ppc
    ck = ppc * page
    hd = DIFF_HEADS * 8
    per_seq = lambda shape: pl.BlockSpec((1,) + shape, lambda s, c, pt: (s, 0, 0))
    vec = lambda w: pl.BlockSpec((1, w), lambda s, c, pt: (0, 0))
    hbm = pl.BlockSpec(memory_space=pl.ANY)
    grid_spec = pltpu.PrefetchScalarGridSpec(
        num_scalar_prefetch=1,
        grid=(ns, nc),
        in_specs=[per_seq((_WQ_END, LANES)), per_seq((NEW_PAD, KV_LORA)), per_seq((NEW_PAD, ROPE_PAD)),
                  per_seq((NEW_PAD, 2 * DIFF_HALF)), per_seq((NEW_PAD, DIFF_V)),
                  vec(DIFF_HALF), vec(DIFF_HALF), vec(DIFF_HALF), vec(DIFF_HALF), vec(DIFF_V),
                  hbm, hbm, hbm, hbm],
        out_specs=[per_seq((_SM_ROWS, KV_LORA)), per_seq((hd, DIFF_V))],
        scratch_shapes=[pltpu.VMEM((2, ppc, page, KV_LORA), F32),
                        pltpu.VMEM((2, MLA_ROPE, ck), F32),
                        pltpu.VMEM((2, ppc, page, 2 * DIFF_HALF), F32),
                        pltpu.VMEM((2, ppc, page, DIFF_V), F32),
                        pltpu.SemaphoreType.DMA((2, 4)),
                        pltpu.VMEM((1, LANES), F32), pltpu.VMEM((1, LANES), F32),
                        pltpu.VMEM((LANES, KV_LORA), F32), pltpu.VMEM((LANES, DIFF_V), F32),
                        pltpu.VMEM((ck, LANES), F32)],
    )
    return pl.pallas_call(
        functools.partial(_samp_body, li=li, ppc=ppc, nc=nc, page=page, past=npages * page,
                          lam_init=lam_init),
        grid_spec=grid_spec,
        out_shape=[jax.ShapeDtypeStruct((ns, _SM_ROWS, KV_LORA), F32),
                   jax.ShapeDtypeStruct((ns, hd, DIFF_V), F32)],
        compiler_params=_params("arbitrary", "arbitrary"),
    )(page_table, wq, nck, nkr, ndk, ndv, lq1, lk1, lq2, lk2, dn,
      cache_ckv, cache_krope, cache_dk, cache_dv)


def _mixout_body(h_ref, ol_ref, conv_ref, diff_ref, wuv_ref, wo_ref, post_ref, o_ref):
    parts = [_dot(ol_ref[hh], wuv_ref[hh]).astype(BF16) for hh in range(MLA_HEADS)]
    mla = jnp.concatenate(parts, axis=1)
    n_mla = mla.shape[1]
    n_conv = n_mla + CONV_WIDTH
    mixed = (_dot(mla, wo_ref[0:n_mla, :]) + _dot(conv_ref[...], wo_ref[n_mla:n_conv, :])
             + _dot(diff_ref[...], wo_ref[n_conv:, :]))
    o_ref[...] = h_ref[...] + _rms(mixed, post_ref[...])


def _mixout(h, o_lat, conv_o, diff_o, wuv, wo, post_g, li, tm):
    m, d = h.shape
    tok = lambda w: pl.BlockSpec((tm, w), lambda i: (i, 0))
    wo_spec = pl.BlockSpec((None,) + wo.shape[1:], lambda i: (li, 0, 0), pipeline_mode=pl.Buffered(1))
    return pl.pallas_call(
        _mixout_body,
        grid=(m // tm,),
        in_specs=[tok(d), pl.BlockSpec((MLA_HEADS, tm, KV_LORA), lambda i: (0, i, 0)),
                  tok(CONV_WIDTH), tok(DIFF_HEADS * DIFF_V),
                  _const_spec(wuv.shape), wo_spec, _const_spec((1, d))],
        out_specs=tok(d),
        out_shape=jax.ShapeDtypeStruct((m, d), F32),
        compiler_params=_params("arbitrary"),
    )(h, o_lat, conv_o, diff_o, wuv, wo, post_g)


def _swap_halves(w):
    half = w.shape[-1] // 2
    return jnp.concatenate([w[..., half:], w[..., :half]], axis=-1)


def _pad_last(w, width):
    return jnp.pad(w, [(0, 0)] * (w.ndim - 1) + [(0, width - w.shape[-1])])


def _pack_w_in(w):
    s = np.cumsum([0, Q_LORA, KV_LORA, MLA_ROPE, CONV_WIDTH, CONV_WIDTH, CONV_WIDTH,
                   DIFF_HEADS * 2 * DIFF_HALF, 2 * DIFF_HALF, DIFF_V])
    cq, ckv, kr, cb, cc, cx, dq, dk, dv = [w[:, s[k]:s[k + 1]] for k in range(9)]
    return jnp.concatenate([cq, ckv, _pad_last(kr, ROPE_PAD), _pad_last(_swap_halves(kr), ROPE_PAD),
                            cb, cc, cx, dq, dk, dv], axis=1).astype(BF16)


def _rope_tables(pos):
    half = MLA_ROPE // 2
    freqs = ROPE_THETA ** (-jnp.arange(half, dtype=F32) / half)
    ang = pos.astype(F32)[:, None] * freqs[None, :]
    cos, sin = jnp.cos(ang), jnp.sin(ang)
    return (jnp.concatenate([cos, cos, cos, cos], axis=1),
            jnp.concatenate([-sin, sin, -sin, sin], axis=1))


def _tile(n, pref):
    for t in pref:
        if n % t == 0:
            return t
    return n


def kernel(x_prompt, x_sample, cache_ckv, cache_krope, cache_diff_k, cache_diff_v, state_conv, page_table, ffn1_pre_g, ffn1_post_g, ffn1_w_gate, ffn1_w_up, ffn1_w_down, mix_pre_g, mix_post_g, w_in, q_norm_g, kv_norm_g, w_uq, w_uk, w_uv, conv_w, diff_lq1, diff_lk1, diff_lq2, diff_lk2, diff_norm_g, w_o, ffn2_pre_g, ffn2_post_g, ffn2_w_gate, ffn2_w_up, ffn2_w_down):
    nb, seq, d = x_prompt.shape
    ns, dseq, _ = x_sample.shape
    depth = w_in.shape[0]
    npages, page = page_table.shape[1], cache_ckv.shape[2]
    past = npages * page
    mp, msamp = nb * seq, ns * dseq
    m = mp + msamp
    assert dseq == 8 and MLA_HEADS * dseq == _SM_ROWS

    tm_ffn = _tile(m, (512, 256, 128))
    tf = _tile(ffn1_w_gate.shape[2], (512, 256, 128))
    tm_mix = _tile(m, (256, 128))
    tq = _tile(seq, (128,))
    tk = _tile(seq, (512, 256, 128))
    ppc = _tile(npages, (32, 16, 8, 4, 2, 1))

    pos = jnp.concatenate([jnp.tile(jnp.arange(seq, dtype=jnp.int32), nb),
                           jnp.tile(past + jnp.arange(dseq, dtype=jnp.int32), ns)])
    tab_c, tab_s = _rope_tables(pos)

    h = jnp.concatenate([x_prompt.reshape(mp, d), x_sample.reshape(msamp, d)], axis=0)
    cache_krope_t = jnp.swapaxes(cache_krope, 2, 3)
    row = lambda v: v.reshape(1, -1)
    f1_wg, f1_wu, f1_wd = ffn1_w_gate.astype(BF16), ffn1_w_up.astype(BF16), ffn1_w_down.astype(BF16)
    f2_wg, f2_wu, f2_wd = ffn2_w_gate.astype(BF16), ffn2_w_up.astype(BF16), ffn2_w_down.astype(BF16)
    wo_all = w_o.astype(BF16)
    states = []
    for li in range(depth):
        lam_init = 0.8 - 0.6 * math.exp(-0.3 * li)
        uq = w_uq[li]
        wqn = uq[:, :, :MLA_NOPE].reshape(Q_LORA, MLA_HEADS * MLA_NOPE).astype(BF16)
        uqr = uq[:, :, MLA_NOPE:]
        wqr = _pad_last(uqr, ROPE_PAD).reshape(Q_LORA, MLA_HEADS * ROPE_PAD).astype(BF16)
        wqs = _pad_last(_swap_halves(uqr), ROPE_PAD).reshape(Q_LORA, MLA_HEADS * ROPE_PAD).astype(BF16)
        wuk = jnp.transpose(w_uk[li], (1, 2, 0)).astype(BF16)
        wuv = jnp.transpose(w_uv[li], (1, 0, 2)).astype(BF16)
        win = _pack_w_in(w_in[li])
        lq1, lk1, lq2, lk2 = row(diff_lq1[li]), row(diff_lk1[li]), row(diff_lq2[li]), row(diff_lk2[li])
        dn = row(diff_norm_g[li])

        h = _ffn(h, row(ffn1_pre_g[li]), f1_wg, f1_wu, f1_wd, row(ffn1_post_g[li]), li, tm_ffn, tf)

        (ql, qr, ckv, kr, ckvb, krb, cb, u, dq, dk, dv, dkb, dvb) = _mixin(
            h, row(mix_pre_g[li]), win, row(q_norm_g[li]), row(kv_norm_g[li]),
            wqn, wqr, wqs, wuk, tab_c, tab_s, tm_mix)

        cw = conv_w[li]
        u_p = u[:mp].reshape(nb, seq, CONV_WIDTH)
        xpad_p = jnp.concatenate([jnp.zeros((nb, 8, CONV_WIDTH), F32), u_p], axis=1)
        conv_p = _conv(xpad_p, cb[:mp].reshape(nb, seq, CONV_WIDTH), cw)
        u_s = u[mp:].reshape(ns, dseq, CONV_WIDTH)
        xs = jnp.concatenate([jnp.zeros((ns, 8 - (CONV_K - 1), CONV_WIDTH), F32), state_conv[li], u_s], axis=1)
        xpad_s = jnp.concatenate([jnp.zeros((1, 8, CONV_WIDTH), F32), xs.reshape(1, ns * 16, CONV_WIDTH)], axis=1)
        cb_s = jnp.concatenate([jnp.zeros((ns, 8, CONV_WIDTH), F32), cb[mp:].reshape(ns, dseq, CONV_WIDTH)], axis=1)
        conv_s = _conv(xpad_s, cb_s.reshape(1, ns * 16, CONV_WIDTH), cw)
        conv_s = conv_s.reshape(ns, 16, CONV_WIDTH)[:, 8:, :]
        conv_o = jnp.concatenate([conv_p.reshape(mp, CONV_WIDTH), conv_s.reshape(msamp, CONV_WIDTH)], axis=0)
        conv_state_p = u_p[:, seq - (CONV_K - 1):, :]
        conv_state_s = xs[:, 16 - (CONV_K - 1):, :]

        ol_p = _prompt_mla(ql, qr, ckvb, krb, nb, seq, tq, tk)
        diff_p = _prompt_diff(dq, dkb, dvb, lq1, lk1, lq2, lk2, dn, lam_init, nb, seq, tq, tk)

        def per_seq_t(a, nh):
            w = a.shape[-1]
            a = a[:, mp:, :].reshape(nh, ns, dseq, w)
            return jnp.transpose(a, (1, 3, 0, 2)).reshape(ns, w, nh * dseq)

        dq_t = per_seq_t(dq, DIFF_HEADS)
        lane_lo = (jnp.arange(2 * DIFF_HALF) < DIFF_HALF)[None, :, None]
        zq = jnp.zeros_like(dq_t)
        wq_m = jnp.concatenate([per_seq_t(ql, MLA_HEADS), per_seq_t(qr, MLA_HEADS)], axis=1)
        wq_m = jnp.concatenate([wq_m, jnp.zeros((ns, _WQ_DK, LANES - _SM_ROWS), BF16)], axis=2)
        wq_d = jnp.concatenate([jnp.zeros((ns, 2 * DIFF_HALF, _SM_ROWS), BF16),
                                jnp.where(lane_lo, dq_t, zq), jnp.where(lane_lo, zq, dq_t)], axis=2)
        wq = jnp.concatenate([wq_m, wq_d], axis=1)

        def new_rows(a):
            a = a[mp:].reshape(ns, dseq, a.shape[-1])
            return jnp.pad(a, ((0, 0), (0, NEW_PAD - dseq), (0, 0)))

        om, od = _sample_attn(page_table, wq, new_rows(ckvb), new_rows(krb), new_rows(dkb), new_rows(dvb),
                              lq1, lk1, lq2, lk2, dn, cache_ckv, cache_krope_t, cache_diff_k, cache_diff_v,
                              li, lam_init, ppc)
        ol_s = jnp.transpose(om.reshape(ns, MLA_HEADS, dseq, KV_LORA), (1, 0, 2, 3))
        ol = jnp.concatenate([ol_p, ol_s.reshape(MLA_HEADS, msamp, KV_LORA).astype(BF16)], axis=1)
        diff_s = jnp.transpose(od.reshape(ns, DIFF_HEADS, dseq, DIFF_V), (0, 2, 1, 3))
        diff_o = jnp.concatenate([diff_p, diff_s.reshape(msamp, DIFF_HEADS * DIFF_V).astype(BF16)], axis=0)

        h = _mixout(h, ol, conv_o, diff_o, wuv, wo_all, row(mix_post_g[li]), li, tm_mix)
        h = _ffn(h, row(ffn2_pre_g[li]), f2_wg, f2_wu, f2_wd, row(ffn2_post_g[li]), li, tm_ffn, tf)
        states.append((ckv, kr, dk, dv, conv_state_p, conv_state_s))

    def stack(k, lo, hi, shape):
        return jnp.stack([s[k][lo:hi].reshape(shape + (s[k].shape[-1],)) for s in states])

    p_shape, s_shape = (nb, seq), (ns, dseq)
    return (h[:mp].reshape(nb, seq, d), h[mp:].reshape(ns, dseq, d),
            stack(0, 0, mp, p_shape), stack(1, 0, mp, p_shape), stack(2, 0, mp, p_shape), stack(3, 0, mp, p_shape),
            jnp.stack([s[4] for s in states]),
            stack(0, mp, m, s_shape), stack(1, mp, m, s_shape), stack(2, mp, m, s_shape), stack(3, mp, m, s_shape),
            jnp.stack([s[5] for s in states]))
```

```python
import functools
import math

import numpy as np
import jax
import jax.numpy as jnp
from jax import lax
from jax.experimental import pallas as pl
from jax.experimental.pallas import tpu as pltpu

F32 = jnp.float32
BF16 = jnp.bfloat16

MLA_HEADS = 8
MLA_NOPE = 128
MLA_ROPE = 64
KV_LORA = 256
Q_LORA = 512
CONV_WIDTH = 512
CONV_K = 3
DIFF_HEADS = 4
DIFF_HALF = 64
DIFF_V = 2 * DIFF_HALF
ROPE_THETA = 10000.0
NORM_EPS = 1e-6
NEG_INF = -1e30
LANES = 128
ROPE_PAD = LANES
VMEM_LIMIT = 56 * 1024 * 1024

MLA_SCALE = (MLA_NOPE + MLA_ROPE) ** -0.5
DIFF_SCALE = DIFF_HALF ** -0.5

_C_CQ = 0
_C_CKV = _C_CQ + Q_LORA
_C_KR = _C_CKV + KV_LORA
_C_KRS = _C_KR + ROPE_PAD
_C_CB = _C_KRS + ROPE_PAD
_C_CC = _C_CB + CONV_WIDTH
_C_CX = _C_CC + CONV_WIDTH
_C_DQ = _C_CX + CONV_WIDTH
_C_DK = _C_DQ + DIFF_HEADS * 2 * DIFF_HALF
_C_DV = _C_DK + 2 * DIFF_HALF
_C_END = _C_DV + DIFF_V


def _params(*sem):
    return pltpu.CompilerParams(dimension_semantics=sem, vmem_limit_bytes=VMEM_LIMIT)


def _rms(x, g):
    ms = jnp.mean(x * x, axis=-1, keepdims=True)
    return x * lax.rsqrt(ms + NORM_EPS) * g


def _dot(a, b):
    return jnp.dot(a, b, preferred_element_type=F32)


def _dot_nt(a, b):
    return lax.dot_general(a, b, (((1,), (1,)), ((), ())), preferred_element_type=F32)


def _dot_tn(a, b):
    return lax.dot_general(a, b, (((0,), (0,)), ((), ())), preferred_element_type=F32)


def _lam(lq1, lk1, lq2, lk2, lam_init):
    a = jnp.sum(lq1 * lk1, axis=-1, keepdims=True)
    b = jnp.sum(lq2 * lk2, axis=-1, keepdims=True)
    return jnp.exp(a) - jnp.exp(b) + lam_init


def _const_spec(shape):
    nd = len(shape)
    return pl.BlockSpec(shape, lambda *_: (0,) * nd, pipeline_mode=pl.Buffered(1))


def _ffn_body(x_ref, pre_ref, wg_ref, wu_ref, wd_ref, post_ref, o_ref, xn_ref, *, nf):
    f = pl.program_id(1)

    @pl.when(f == 0)
    def _():
        xn_ref[...] = _rms(x_ref[...], pre_ref[...]).astype(BF16)
        o_ref[...] = jnp.zeros_like(o_ref)

    xn = xn_ref[...]
    g = _dot(xn, wg_ref[...])
    u = _dot(xn, wu_ref[...])
    a = (g * (1.0 / (1.0 + jnp.exp(-g)))) * u
    o_ref[...] += _dot(a.astype(BF16), wd_ref[...])

    @pl.when(f == nf - 1)
    def _():
        o_ref[...] = x_ref[...] + 0.5 * _rms(o_ref[...], post_ref[...])


def _ffn(x, pre_g, wg, wu, wd, post_g, li, tm, tf):
    m, d = x.shape
    nf = wg.shape[2] // tf
    return pl.pallas_call(
        functools.partial(_ffn_body, nf=nf),
        grid=(m // tm, nf),
        in_specs=[
            pl.BlockSpec((tm, d), lambda i, f: (i, 0)),
            pl.BlockSpec((1, d), lambda i, f: (0, 0)),
            pl.BlockSpec((None, d, tf), lambda i, f: (li, 0, f)),
            pl.BlockSpec((None, d, tf), lambda i, f: (li, 0, f)),
            pl.BlockSpec((None, tf, d), lambda i, f: (li, f, 0)),
            pl.BlockSpec((1, d), lambda i, f: (0, 0)),
        ],
        out_specs=pl.BlockSpec((tm, d), lambda i, f: (i, 0)),
        out_shape=jax.ShapeDtypeStruct((m, d), F32),
        scratch_shapes=[pltpu.VMEM((tm, d), BF16)],
        compiler_params=_params("arbitrary", "arbitrary"),
    )(x, pre_g, wg, wu, wd, post_g)


def _mixin_body(h_ref, pre_ref, win_ref, qn_ref, kvn_ref, wqn_ref, wqr_ref, wqs_ref, wuk_ref,
                tc_ref, ts_ref,
                ql_ref, qr_ref, ckv_ref, kr_ref, ckvb_ref, krb_ref, cb_ref, u_ref,
                dq_ref, dk_ref, dv_ref, dkb_ref, dvb_ref):
    n = _rms(h_ref[...], pre_ref[...]).astype(BF16)
    tab_c = tc_ref[...]
    tab_s = ts_ref[...]

    def proj(c0, c1):
        return _dot(n, win_ref[:, c0:c1])

    cq = _rms(proj(_C_CQ, _C_CKV), qn_ref[...]).astype(BF16)
    qn = _dot(cq, wqn_ref[...]).astype(BF16)
    qr = _dot(cq, wqr_ref[...])
    qs = _dot(cq, wqs_ref[...])
    for hh in range(MLA_HEADS):
        q_lat = _dot(qn[:, hh * MLA_NOPE:(hh + 1) * MLA_NOPE], wuk_ref[hh])
        ql_ref[hh] = (q_lat * MLA_SCALE).astype(BF16)
        sl = slice(hh * ROPE_PAD, (hh + 1) * ROPE_PAD)
        qr_ref[hh] = ((qr[:, sl] * tab_c + qs[:, sl] * tab_s) * MLA_SCALE).astype(BF16)
    ckv = _rms(proj(_C_CKV, _C_KR), kvn_ref[...])
    ckv_ref[...] = ckv
    ckvb_ref[...] = ckv.astype(BF16)
    kr = proj(_C_KR, _C_KRS) * tab_c + proj(_C_KRS, _C_CB) * tab_s
    kr_ref[...] = kr[:, :MLA_ROPE]
    krb_ref[...] = kr.astype(BF16)
    cb_ref[...] = proj(_C_CB, _C_CC)
    u_ref[...] = proj(_C_CC, _C_CX) * proj(_C_CX, _C_DQ)
    dq = proj(_C_DQ, _C_DK) * DIFF_SCALE
    for hh in range(DIFF_HEADS):
        dq_ref[hh] = dq[:, hh * 2 * DIFF_HALF:(hh + 1) * 2 * DIFF_HALF].astype(BF16)
    dk = proj(_C_DK, _C_DV)
    dv = proj(_C_DV, _C_END)
    dk_ref[...] = dk
    dv_ref[...] = dv
    dkb_ref[...] = dk.astype(BF16)
    dvb_ref[...] = dv.astype(BF16)


def _mixin(h, pre_g, win, qn_g, kvn_g, wqn, wqr, wqs, wuk, tab_c, tab_s, tm):
    m, d = h.shape
    tok = lambda w: pl.BlockSpec((tm, w), lambda i: (i, 0))
    headmajor = lambda nh, w: pl.BlockSpec((nh, tm, w), lambda i: (0, i, 0))
    sds = jax.ShapeDtypeStruct
    return pl.pallas_call(
        _mixin_body,
        grid=(m // tm,),
        in_specs=[tok(d), _const_spec((1, d)), _const_spec(win.shape), _const_spec((1, Q_LORA)),
                  _const_spec((1, KV_LORA)), _const_spec(wqn.shape), _const_spec(wqr.shape),
                  _const_spec(wqs.shape), _const_spec(wuk.shape), tok(LANES), tok(LANES)],
        out_specs=[headmajor(MLA_HEADS, KV_LORA), headmajor(MLA_HEADS, ROPE_PAD),
                   tok(KV_LORA), tok(MLA_ROPE), tok(KV_LORA), tok(ROPE_PAD),
                   tok(CONV_WIDTH), tok(CONV_WIDTH),
                   headmajor(DIFF_HEADS, 2 * DIFF_HALF), tok(2 * DIFF_HALF), tok(DIFF_V),
                   tok(2 * DIFF_HALF), tok(DIFF_V)],
        out_shape=[sds((MLA_HEADS, m, KV_LORA), BF16), sds((MLA_HEADS, m, ROPE_PAD), BF16),
                   sds((m, KV_LORA), F32), sds((m, MLA_ROPE), F32),
                   sds((m, KV_LORA), BF16), sds((m, ROPE_PAD), BF16),
                   sds((m, CONV_WIDTH), F32), sds((m, CONV_WIDTH), F32),
                   sds((DIFF_HEADS, m, 2 * DIFF_HALF), BF16),
                   sds((m, 2 * DIFF_HALF), F32), sds((m, DIFF_V), F32),
                   sds((m, 2 * DIFF_HALF), BF16), sds((m, DIFF_V), BF16)],
        compiler_params=_params("arbitrary"),
    )(h, pre_g, win, qn_g, kvn_g, wqn, wqr, wqs, wuk, tab_c, tab_s)


def _conv_body(xp_ref, cb_ref, cw_ref, o_ref):
    t = cb_ref.shape[1]
    conv = (cw_ref[0:1, :] * xp_ref[0, 6:6 + t, :]
            + cw_ref[1:2, :] * xp_ref[0, 7:7 + t, :]
            + cw_ref[2:3, :] * xp_ref[0, 8:8 + t, :])
    o_ref[0] = (cb_ref[0] * conv).astype(BF16)


def _conv(xpad, cb, cw):
    b, t, c = cb.shape
    return pl.pallas_call(
        _conv_body,
        grid=(b, c // LANES),
        in_specs=[pl.BlockSpec((1, t + 8, LANES), lambda i, j: (i, 0, j)),
                  pl.BlockSpec((1, t, LANES), lambda i, j: (i, 0, j)),
                  pl.BlockSpec((CONV_K, LANES), lambda i, j: (0, j))],
        out_specs=pl.BlockSpec((1, t, LANES), lambda i, j: (i, 0, j)),
        out_shape=jax.ShapeDtypeStruct((b, t, c), BF16),
        compiler_params=_params("arbitrary", "arbitrary"),
    )(xpad, cb, cw)


def _causal_softmax_pv(i, tq, tk, score, v_ref, s_s, mb_s, l_s, acc_s):
    rows = acc_s.shape[0]
    nblk = tk // LANES
    nfull = (i * tq + 1) // tk
    nneed = ((i + 1) * tq + tk - 1) // tk
    mb_s[...] = jnp.full_like(mb_s, NEG_INF)

    def pass1(masked):
        def body(j, c):
            k0 = pl.multiple_of(j * tk, tk)
            s = score(j, k0, masked)
            if masked:
                t = lax.broadcasted_iota(jnp.int32, (rows, tk), 0) % tq
                col = lax.broadcasted_iota(jnp.int32, (rows, tk), 1)
                s = jnp.where(col + k0 <= t + i * tq, s, NEG_INF)
            s_s[j] = s
            mr = mb_s[...]
            for b in range(nblk):
                mr = jnp.maximum(mr, s[:, b * LANES:(b + 1) * LANES])
            mb_s[...] = mr
            return c
        return body

    lax.fori_loop(0, nfull, pass1(False), 0)
    lax.fori_loop(nfull, nneed, pass1(True), 0)
    m = jnp.max(mb_s[...], axis=1, keepdims=True)
    mb_s[...] = jnp.broadcast_to(m, mb_s.shape)
    l_s[...] = jnp.zeros_like(l_s)
    acc_s[...] = jnp.zeros_like(acc_s)

    def pass2(j, c):
        k0 = pl.multiple_of(j * tk, tk)
        mb = mb_s[...]
        lr = l_s[...]
        parts = []
        for b in range(nblk):
            pb = jnp.exp(s_s[j, :, b * LANES:(b + 1) * LANES] - mb)
            lr = lr + pb
            parts.append(pb.astype(BF16))
        l_s[...] = lr
        acc_s[...] += _dot(jnp.concatenate(parts, axis=1), v_ref[pl.ds(k0, tk), :])
        return c

    lax.fori_loop(0, nneed, pass2, 0)
    return acc_s[...] / jnp.sum(l_s[...], axis=1, keepdims=True)


def _pmla_body(ql_ref, qr_ref, ckv_ref, kr_ref, o_ref, s_s, mb_s, l_s, acc_s, *, tq, tk):
    i = pl.program_id(1)
    rows = MLA_HEADS * tq
    q = ql_ref[...].reshape(rows, KV_LORA)
    qr = qr_ref[...].reshape(rows, ROPE_PAD)

    def score(j, k0, masked):
        return _dot_nt(q, ckv_ref[pl.ds(k0, tk), :]) + _dot_nt(qr, kr_ref[pl.ds(k0, tk), :])

    o = _causal_softmax_pv(i, tq, tk, score, ckv_ref, s_s, mb_s, l_s, acc_s)
    o_ref[...] = o.astype(BF16).reshape(MLA_HEADS, tq, KV_LORA)


def _prompt_mla(ql, qr, ckvb, krb, nb, seq, tq, tk):
    nq = seq // tq
    rows = MLA_HEADS * tq
    return pl.pallas_call(
        functools.partial(_pmla_body, tq=tq, tk=tk),
        grid=(nb, nq),
        in_specs=[pl.BlockSpec((MLA_HEADS, tq, KV_LORA), lambda b, i: (0, b * nq + i, 0)),
                  pl.BlockSpec((MLA_HEADS, tq, ROPE_PAD), lambda b, i: (0, b * nq + i, 0)),
                  pl.BlockSpec((seq, KV_LORA), lambda b, i: (b, 0)),
                  pl.BlockSpec((seq, ROPE_PAD), lambda b, i: (b, 0))],
        out_specs=pl.BlockSpec((MLA_HEADS, tq, KV_LORA), lambda b, i: (0, b * nq + i, 0)),
        out_shape=jax.ShapeDtypeStruct((MLA_HEADS, nb * seq, KV_LORA), BF16),
        scratch_shapes=[pltpu.VMEM((seq // tk, rows, tk), F32), pltpu.VMEM((rows, LANES), F32),
                        pltpu.VMEM((rows, LANES), F32), pltpu.VMEM((rows, KV_LORA), F32)],
        compiler_params=_params("arbitrary", "arbitrary"),
    )(ql, qr, ckvb, krb)


def _head_slope(h):
    return jnp.where(h == 0, 2.0 ** -2, jnp.where(h == 1, 2.0 ** -4,
                                                   jnp.where(h == 2, 2.0 ** -6, 2.0 ** -8))).astype(F32)


def _pdiff_body(dq_ref, k_ref, v_ref, lq1_ref, lk1_ref, lq2_ref, lk2_ref, dn_ref, o_ref,
                s_s, mb_s, l_s, acc_s, b0_s, *, tq, tk, lam_init):
    i = pl.program_id(1)
    half = DIFF_HEADS * tq
    rows = 2 * half

    @pl.when((pl.program_id(0) == 0) & (i == 0))
    def _():
        r = lax.broadcasted_iota(jnp.int32, (rows, tk), 0)
        col = lax.broadcasted_iota(jnp.int32, (rows, tk), 1)
        b0_s[...] = _head_slope((r // tq) % DIFF_HEADS) * (col - r % tq).astype(F32)

    x = dq_ref[...].reshape(half, 2 * DIFF_HALF)
    lane = lax.broadcasted_iota(jnp.int32, x.shape, 1)
    zero = jnp.zeros_like(x)
    q = jnp.concatenate([jnp.where(lane < DIFF_HALF, x, zero),
                         jnp.where(lane >= DIFF_HALF, x, zero)], axis=0)
    slope = _head_slope((lax.broadcasted_iota(jnp.int32, (rows, LANES), 0) // tq) % DIFF_HEADS)

    def score(j, k0, masked):
        off = slope * (k0 - i * tq).astype(F32)
        return (_dot_nt(q, k_ref[pl.ds(k0, tk), :]) + b0_s[...]
                + jnp.concatenate([off] * (tk // LANES), axis=1))

    o = _causal_softmax_pv(i, tq, tk, score, v_ref, s_s, mb_s, l_s, acc_s)
    lam = _lam(lq1_ref[...], lk1_ref[...], lq2_ref[...], lk2_ref[...], lam_init)
    od = _rms(o[:half] - lam * o[half:], dn_ref[...]) * (1.0 - lam_init)
    for hh in range(DIFF_HEADS):
        o_ref[:, hh * DIFF_V:(hh + 1) * DIFF_V] = od[hh * tq:(hh + 1) * tq].astype(BF16)


def _prompt_diff(dq, dkb, dvb, lq1, lk1, lq2, lk2, dn, lam_init, nb, seq, tq, tk):
    nq = seq // tq
    rows = 2 * DIFF_HEADS * tq
    vec = lambda w: pl.BlockSpec((1, w), lambda b, i: (0, 0))
    return pl.pallas_call(
        functools.partial(_pdiff_body, tq=tq, tk=tk, lam_init=lam_init),
        grid=(nb, nq),
        in_specs=[pl.BlockSpec((DIFF_HEADS, tq, 2 * DIFF_HALF), lambda b, i: (0, b * nq + i, 0)),
                  pl.BlockSpec((seq, 2 * DIFF_HALF), lambda b, i: (b, 0)),
                  pl.BlockSpec((seq, DIFF_V), lambda b, i: (b, 0)),
                  vec(DIFF_HALF), vec(DIFF_HALF), vec(DIFF_HALF), vec(DIFF_HALF), vec(DIFF_V)],
        out_specs=pl.BlockSpec((tq, DIFF_HEADS * DIFF_V), lambda b, i: (b * nq + i, 0)),
        out_shape=jax.ShapeDtypeStruct((nb * seq, DIFF_HEADS * DIFF_V), BF16),
        scratch_shapes=[pltpu.VMEM((seq // tk, rows, tk), F32), pltpu.VMEM((rows, LANES), F32),
                        pltpu.VMEM((rows, LANES), F32), pltpu.VMEM((rows, DIFF_V), F32),
                        pltpu.VMEM((rows, tk), F32)],
        compiler_params=_params("arbitrary", "arbitrary"),
    )(dq, dkb, dvb, lq1, lk1, lq2, lk2, dn)


_SM_ROWS = MLA_HEADS * 8
_WQ_KR = KV_LORA
_WQ_DK = KV_LORA + ROPE_PAD
_WQ_END = _WQ_DK + 2 * DIFF_HALF
NEW_PAD = 16


def _lane_slope(lane):
    return jnp.where(lane >= _SM_ROWS, _head_slope(((lane - _SM_ROWS) // 8) % DIFF_HEADS), 0.0)


def _to_col(row):
    r = lax.broadcasted_iota(jnp.int32, (LANES, LANES), 0)
    c = lax.broadcasted_iota(jnp.int32, (LANES, LANES), 1)
    return jnp.sum(jnp.where(r == c, jnp.broadcast_to(row, (LANES, LANES)), 0.0), axis=1, keepdims=True)


def _samp_body(pt_ref, wq_ref, nck_ref, nkr_ref, ndk_ref, ndv_ref,
               lq1_ref, lk1_ref, lq2_ref, lk2_ref, dn_ref,
               cckv, ckr, cdk, cdv, om_ref, od_ref,
               bckv, bkr, bdk, bdv, sems, m_s, l_s, accm_s, accd_s, b0_s,
               *, li, ppc, nc, page, past, lam_init):
    s_id = pl.program_id(0)
    c_id = pl.program_id(1)
    ns = pl.num_programs(0)
    g = s_id * nc + c_id
    slot = g % 2
    ck = ppc * page
    caches = (cckv, ckr, cdk, cdv)
    bufs = (bckv, bkr, bdk, bdv)

    def copies(s, c, sl, from_table):
        out = []
        base = (s * nc + c) * ppc
        for p in range(ppc):
            pg = pt_ref[base + p] if from_table else 0
            for a in range(4):
                dst = bkr.at[sl, :, pl.ds(p * page, page)] if a == 1 else bufs[a].at[sl, p]
                out.append(pltpu.make_async_copy(caches[a].at[li, pg], dst, sems.at[sl, a]))
        return out

    def start_all(cps):
        for n, cp in enumerate(cps):
            cp.start(priority=(n // 4) % 2)

    @pl.when(g == 0)
    def _():
        start_all(copies(s_id, c_id, slot, True))
        key = lax.broadcasted_iota(jnp.int32, (ck, LANES), 0)
        lane = lax.broadcasted_iota(jnp.int32, (ck, LANES), 1)
        b0_s[...] = _lane_slope(lane) * key.astype(F32)

    @pl.when(g + 1 < ns * nc)
    def _():
        last = c_id == nc - 1
        s_n = jnp.where(last, s_id + 1, s_id)
        c_n = jnp.where(last, 0, c_id + 1)
        start_all(copies(s_n, c_n, 1 - slot, True))

    for cp in copies(0, 0, slot, False):
        cp.wait()

    @pl.when(c_id == 0)
    def _():
        m_s[...] = jnp.full_like(m_s, NEG_INF)
        l_s[...] = jnp.zeros_like(l_s)
        accm_s[...] = jnp.zeros_like(accm_s)
        accd_s[...] = jnp.zeros_like(accd_s)

    lane1 = lax.broadcasted_iota(jnp.int32, (1, LANES), 1)
    slope1 = _lane_slope(lane1)
    tok1 = lane1 % 8

    def update(state, st, off, kv, vd):
        m_prev, l_prev, acc_m, acc_d = state
        m_new = jnp.maximum(m_prev, jnp.max(st, axis=0, keepdims=True) + off)
        alpha = jnp.exp(m_prev - m_new)
        p = jnp.exp(st - (m_new - off))
        l_new = alpha * l_prev + jnp.sum(p, axis=0, keepdims=True)
        pb = p.astype(BF16)
        a_col = _to_col(alpha)
        return (m_new, l_new, a_col * acc_m + _dot_tn(pb, kv), a_col * acc_d + _dot_tn(pb, vd))

    def load_state():
        return (m_s[...], l_s[...], accm_s[...], accd_s[...])

    kc = bckv[slot].reshape(ck, KV_LORA).astype(BF16)
    krt = bkr[slot].astype(BF16)
    kdc = bdk[slot].reshape(ck, 2 * DIFF_HALF).astype(BF16)
    vdc = bdv[slot].reshape(ck, DIFF_V).astype(BF16)
    st = (_dot(kc, wq_ref[0, 0:_WQ_KR, :]) + _dot_tn(krt, wq_ref[0, _WQ_KR:_WQ_KR + MLA_ROPE, :])
          + _dot(kdc, wq_ref[0, _WQ_DK:_WQ_END, :]) + b0_s[...])
    off = slope1 * (c_id * ck - past - tok1).astype(F32)
    m_s[...], l_s[...], accm_s[...], accd_s[...] = update(load_state(), st, off, kc, vdc)

    @pl.when(c_id == nc - 1)
    def _():
        nck = nck_ref[0]
        ndv = ndv_ref[0]
        sn = (_dot(nck, wq_ref[0, 0:_WQ_KR, :]) + _dot(nkr_ref[0], wq_ref[0, _WQ_KR:_WQ_DK, :])
              + _dot(ndk_ref[0], wq_ref[0, _WQ_DK:_WQ_END, :]))
        j = lax.broadcasted_iota(jnp.int32, (NEW_PAD, LANES), 0)
        lane = lax.broadcasted_iota(jnp.int32, (NEW_PAD, LANES), 1)
        tok = lane % 8
        sn = sn + _lane_slope(lane) * (j - tok).astype(F32)
        sn = jnp.where(j <= tok, sn, NEG_INF)
        _, l_fin, acc_m, acc_d = update(load_state(), sn, jnp.zeros((1, LANES), F32), nck, ndv)
        l_col = _to_col(l_fin)
        om_ref[0] = acc_m[0:_SM_ROWS, :] / l_col[0:_SM_ROWS]
        od_all = acc_d / l_col
        hd = DIFF_HEADS * 8
        lam = _lam(lq1_ref[...], lk1_ref[...], lq2_ref[...], lk2_ref[...], lam_init)
        od = od_all[_SM_ROWS:_SM_ROWS + hd] - lam * od_all[_SM_ROWS + hd:_SM_ROWS + 2 * hd]
        od_ref[0] = _rms(od, dn_ref[...]) * (1.0 - lam_init)


def _sample_attn(page_table, wq, nck, nkr, ndk, ndv, lq1, lk1, lq2, lk2, dn,
                 cache_ckv, cache_krope, cache_dk, cache_dv, li, lam_init, ppc):
    ns, npages = page_table.shape
    page = cache_ckv.shape[2]
    nc = npages // ppc
    ck = ppc * page
    hd = DIFF_HEADS * 8
    per_seq = lambda shape: pl.BlockSpec((1,) + shape, lambda s, c, pt: (s, 0, 0))
    vec = lambda w: pl.BlockSpec((1, w), lambda s, c, pt: (0, 0))
    hbm = pl.BlockSpec(memory_space=pl.ANY)
    grid_spec = pltpu.PrefetchScalarGridSpec(
        num_scalar_prefetch=1,
        grid=(ns, nc),
        in_specs=[per_seq((_WQ_END, LANES)), per_seq((NEW_PAD, KV_LORA)), per_seq((NEW_PAD, ROPE_PAD)),
                  per_seq((NEW_PAD, 2 * DIFF_HALF)), per_seq((NEW_PAD, DIFF_V)),
                  vec(DIFF_HALF), vec(DIFF_HALF), vec(DIFF_HALF), vec(DIFF_HALF), vec(DIFF_V),
                  hbm, hbm, hbm, hbm],
        out_specs=[per_seq((_SM_ROWS, KV_LORA)), per_seq((hd, DIFF_V))],
        scratch_shapes=[pltpu.VMEM((2, ppc, page, KV_LORA), F32),
                        pltpu.VMEM((2, MLA_ROPE, ck), F32),
                        pltpu.VMEM((2, ppc, page, 2 * DIFF_HALF), F32),
                        pltpu.VMEM((2, ppc, page, DIFF_V), F32),
                        pltpu.SemaphoreType.DMA((2, 4)),
                        pltpu.VMEM((1, LANES), F32), pltpu.VMEM((1, LANES), F32),
                        pltpu.VMEM((LANES, KV_LORA), F32), pltpu.VMEM((LANES, DIFF_V), F32),
                        pltpu.VMEM((ck, LANES), F32)],
    )
    return pl.pallas_call(
        functools.partial(_samp_body, li=li, ppc=ppc, nc=nc, page=page, past=npages * page,
                          lam_init=lam_init),
        grid_spec=grid_spec,
        out_shape=[jax.ShapeDtypeStruct((ns, _SM_ROWS, KV_LORA), F32),
                   jax.ShapeDtypeStruct((ns, hd, DIFF_V), F32)],
        compiler_params=_params("arbitrary", "arbitrary"),
    )(page_table.reshape(-1), wq, nck, nkr, ndk, ndv, lq1, lk1, lq2, lk2, dn,
      cache_ckv, cache_krope, cache_dk, cache_dv)


def _mixout_body(h_ref, ol_ref, conv_ref, diff_ref, wuv_ref, wo_ref, post_ref, o_ref):
    parts = [_dot(ol_ref[hh], wuv_ref[hh]).astype(BF16) for hh in range(MLA_HEADS)]
    mla = jnp.concatenate(parts, axis=1)
    n_mla = mla.shape[1]
    n_conv = n_mla + CONV_WIDTH
    mixed = (_dot(mla, wo_ref[0:n_mla, :]) + _dot(conv_ref[...], wo_ref[n_mla:n_conv, :])
             + _dot(diff_ref[...], wo_ref[n_conv:, :]))
    o_ref[...] = h_ref[...] + _rms(mixed, post_ref[...])


def _mixout(h, o_lat, conv_o, diff_o, wuv, wo, post_g, li, tm):
    m, d = h.shape
    tok = lambda w: pl.BlockSpec((tm, w), lambda i: (i, 0))
    wo_spec = pl.BlockSpec((None,) + wo.shape[1:], lambda i: (li, 0, 0), pipeline_mode=pl.Buffered(1))
    return pl.pallas_call(
        _mixout_body,
        grid=(m // tm,),
        in_specs=[tok(d), pl.BlockSpec((MLA_HEADS, tm, KV_LORA), lambda i: (0, i, 0)),
                  tok(CONV_WIDTH), tok(DIFF_HEADS * DIFF_V),
                  _const_spec(wuv.shape), wo_spec, _const_spec((1, d))],
        out_specs=tok(d),
        out_shape=jax.ShapeDtypeStruct((m, d), F32),
        compiler_params=_params("arbitrary"),
    )(h, o_lat, conv_o, diff_o, wuv, wo, post_g)


def _swap_halves(w):
    half = w.shape[-1] // 2
    return jnp.concatenate([w[..., half:], w[..., :half]], axis=-1)


def _pad_last(w, width):
    return jnp.pad(w, [(0, 0)] * (w.ndim - 1) + [(0, width - w.shape[-1])])


def _pack_w_in(w):
    s = np.cumsum([0, Q_LORA, KV_LORA, MLA_ROPE, CONV_WIDTH, CONV_WIDTH, CONV_WIDTH,
                   DIFF_HEADS * 2 * DIFF_HALF, 2 * DIFF_HALF, DIFF_V])
    cq, ckv, kr, cb, cc, cx, dq, dk, dv = [w[:, s[k]:s[k + 1]] for k in range(9)]
    return jnp.concatenate([cq, ckv, _pad_last(kr, ROPE_PAD), _pad_last(_swap_halves(kr), ROPE_PAD),
                            cb, cc, cx, dq, dk, dv], axis=1).astype(BF16)


def _rope_tables(pos):
    half = MLA_ROPE // 2
    freqs = ROPE_THETA ** (-jnp.arange(half, dtype=F32) / half)
    ang = pos.astype(F32)[:, None] * freqs[None, :]
    cos, sin = jnp.cos(ang), jnp.sin(ang)
    return (jnp.concatenate([cos, cos, cos, cos], axis=1),
            jnp.concatenate([-sin, sin, -sin, sin], axis=1))


def _tile(n, pref):
    for t in pref:
        if n % t == 0:
            return t
    return n


def kernel(x_prompt, x_sample, cache_ckv, cache_krope, cache_diff_k, cache_diff_v, state_conv, page_table, ffn1_pre_g, ffn1_post_g, ffn1_w_gate, ffn1_w_up, ffn1_w_down, mix_pre_g, mix_post_g, w_in, q_norm_g, kv_norm_g, w_uq, w_uk, w_uv, conv_w, diff_lq1, diff_lk1, diff_lq2, diff_lk2, diff_norm_g, w_o, ffn2_pre_g, ffn2_post_g, ffn2_w_gate, ffn2_w_up, ffn2_w_down):
    nb, seq, d = x_prompt.shape
    ns, dseq, _ = x_sample.shape
    depth = w_in.shape[0]
    npages, page = page_table.shape[1], cache_ckv.shape[2]
    past = npages * page
    mp, msamp = nb * seq, ns * dseq
    m = mp + msamp
    assert dseq == 8 and MLA_HEADS * dseq == _SM_ROWS

    tm_ffn = _tile(m, (512, 256, 128))
    tf = _tile(ffn1_w_gate.shape[2], (512, 256, 128))
    tm_mix = _tile(m, (256, 128))
    tq = _tile(seq, (128,))
    tk = _tile(seq, (512, 256, 128))
    ppc = _tile(npages, (32, 16, 8, 4, 2, 1))

    pos = jnp.concatenate([jnp.tile(jnp.arange(seq, dtype=jnp.int32), nb),
                           jnp.tile(past + jnp.arange(dseq, dtype=jnp.int32), ns)])
    tab_c, tab_s = _rope_tables(pos)

    h = jnp.concatenate([x_prompt.reshape(mp, d), x_sample.reshape(msamp, d)], axis=0)
    cache_krope_t = jnp.swapaxes(cache_krope, 2, 3)
    row = lambda v: v.reshape(1, -1)
    f1_wg, f1_wu, f1_wd = ffn1_w_gate.astype(BF16), ffn1_w_up.astype(BF16), ffn1_w_down.astype(BF16)
    f2_wg, f2_wu, f2_wd = ffn2_w_gate.astype(BF16), ffn2_w_up.astype(BF16), ffn2_w_down.astype(BF16)
    wo_all = w_o.astype(BF16)
    states = []
    for li in range(depth):
        lam_init = 0.8 - 0.6 * math.exp(-0.3 * li)
        uq = w_uq[li]
        wqn = uq[:, :, :MLA_NOPE].reshape(Q_LORA, MLA_HEADS * MLA_NOPE).astype(BF16)
        uqr = uq[:, :, MLA_NOPE:]
        wqr = _pad_last(uqr, ROPE_PAD).reshape(Q_LORA, MLA_HEADS * ROPE_PAD).astype(BF16)
        wqs = _pad_last(_swap_halves(uqr), ROPE_PAD).reshape(Q_LORA, MLA_HEADS * ROPE_PAD).astype(BF16)
        wuk = jnp.transpose(w_uk[li], (1, 2, 0)).astype(BF16)
        wuv = jnp.transpose(w_uv[li], (1, 0, 2)).astype(BF16)
        win = _pack_w_in(w_in[li])
        lq1, lk1, lq2, lk2 = row(diff_lq1[li]), row(diff_lk1[li]), row(diff_lq2[li]), row(diff_lk2[li])
        dn = row(diff_norm_g[li])

        h = _ffn(h, row(ffn1_pre_g[li]), f1_wg, f1_wu, f1_wd, row(ffn1_post_g[li]), li, tm_ffn, tf)

        (ql, qr, ckv, kr, ckvb, krb, cb, u, dq, dk, dv, dkb, dvb) = _mixin(
            h, row(mix_pre_g[li]), win, row(q_norm_g[li]), row(kv_norm_g[li]),
            wqn, wqr, wqs, wuk, tab_c, tab_s, tm_mix)

        cw = conv_w[li]
        u_p = u[:mp].reshape(nb, seq, CONV_WIDTH)
        xpad_p = jnp.concatenate([jnp.zeros((nb, 8, CONV_WIDTH), F32), u_p], axis=1)
        conv_p = _conv(xpad_p, cb[:mp].reshape(nb, seq, CONV_WIDTH), cw)
        u_s = u[mp:].reshape(ns, dseq, CONV_WIDTH)
        xs = jnp.concatenate([jnp.zeros((ns, 8 - (CONV_K - 1), CONV_WIDTH), F32), state_conv[li], u_s], axis=1)
        xpad_s = jnp.concatenate([jnp.zeros((1, 8, CONV_WIDTH), F32), xs.reshape(1, ns * 16, CONV_WIDTH)], axis=1)
        cb_s = jnp.concatenate([jnp.zeros((ns, 8, CONV_WIDTH), F32), cb[mp:].reshape(ns, dseq, CONV_WIDTH)], axis=1)
        conv_s = _conv(xpad_s, cb_s.reshape(1, ns * 16, CONV_WIDTH), cw)
        conv_s = conv_s.reshape(ns, 16, CONV_WIDTH)[:, 8:, :]
        conv_o = jnp.concatenate([conv_p.reshape(mp, CONV_WIDTH), conv_s.reshape(msamp, CONV_WIDTH)], axis=0)
        conv_state_p = u_p[:, seq - (CONV_K - 1):, :]
        conv_state_s = xs[:, 16 - (CONV_K - 1):, :]

        ol_p = _prompt_mla(ql, qr, ckvb, krb, nb, seq, tq, tk)
        diff_p = _prompt_diff(dq, dkb, dvb, lq1, lk1, lq2, lk2, dn, lam_init, nb, seq, tq, tk)

        def per_seq_t(a, nh):
            w = a.shape[-1]
            a = a[:, mp:, :].reshape(nh, ns, dseq, w)
            return jnp.transpose(a, (1, 3, 0, 2)).reshape(ns, w, nh * dseq)

        dq_t = per_seq_t(dq, DIFF_HEADS)
        lane_lo = (jnp.arange(2 * DIFF_HALF) < DIFF_HALF)[None, :, None]
        zq = jnp.zeros_like(dq_t)
        wq_m = jnp.concatenate([per_seq_t(ql, MLA_HEADS), per_seq_t(qr, MLA_HEADS)], axis=1)
        wq_m = jnp.concatenate([wq_m, jnp.zeros((ns, _WQ_DK, LANES - _SM_ROWS), BF16)], axis=2)
        wq_d = jnp.concatenate([jnp.zeros((ns, 2 * DIFF_HALF, _SM_ROWS), BF16),
                                jnp.where(lane_lo, dq_t, zq), jnp.where(lane_lo, zq, dq_t)], axis=2)
        wq = jnp.concatenate([wq_m, wq_d], axis=1)

        def new_rows(a):
            a = a[mp:].reshape(ns, dseq, a.shape[-1])
            return jnp.pad(a, ((0, 0), (0, NEW_PAD - dseq), (0, 0)))

        om, od = _sample_attn(page_table, wq, new_rows(ckvb), new_rows(krb), new_rows(dkb), new_rows(dvb),
                              lq1, lk1, lq2, lk2, dn, cache_ckv, cache_krope_t, cache_diff_k, cache_diff_v,
                              li, lam_init, ppc)
        ol_s = jnp.transpose(om.reshape(ns, MLA_HEADS, dseq, KV_LORA), (1, 0, 2, 3))
        ol = jnp.concatenate([ol_p, ol_s.reshape(MLA_HEADS, msamp, KV_LORA).astype(BF16)], axis=1)
        diff_s = jnp.transpose(od.reshape(ns, DIFF_HEADS, dseq, DIFF_V), (0, 2, 1, 3))
        diff_o = jnp.concatenate([diff_p, diff_s.reshape(msamp, DIFF_HEADS * DIFF_V).astype(BF16)], axis=0)

        h = _mixout(h, ol, conv_o, diff_o, wuv, wo_all, row(mix_post_g[li]), li, tm_mix)
        h = _ffn(h, row(ffn2_pre_g[li]), f2_wg, f2_wu, f2_wd, row(ffn2_post_g[li]), li, tm_ffn, tf)
        states.append((ckv, kr, dk, dv, conv_state_p, conv_state_s))

    def stack(k, lo, hi, shape):
        return jnp.stack([s[k][lo:hi].reshape(shape + (s[k].shape[-1],)) for s in states])

    p_shape, s_shape = (nb, seq), (ns, dseq)
    return (h[:mp].reshape(nb, seq, d), h[mp:].reshape(ns, dseq, d),
            stack(0, 0, mp, p_shape), stack(1, 0, mp, p_shape), stack(2, 0, mp, p_shape), stack(3, 0, mp, p_shape),
            jnp.stack([s[4] for s in states]),
            stack(0, mp, m, s_shape), stack(1, mp, m, s_shape), stack(2, mp, m, s_shape), stack(3, mp, m, s_shape),
            jnp.stack([s[5] for s in states]))
```
